```python
import math
import jax
import jax.numpy as jnp
from jax import lax
import numpy as np

D_MODEL = 1024
BATCH = 16
SEQ = 2048
DEPTH = 2

A_HEADS = 4
A_DIM = 64
B_GROUPS = 4
B_GROUP_DIM = 128
B_CHUNK = 128
C_HEADS = 4
C_DIM = 128
C_IDX_HEADS = 8
C_IDX_DIM = 64
C_TOPK_MAX = 256
MEM_LEN = 256
M_HEADS = 4
M_DIM = 128
P_HEADS = 8
P_QDIM = 256
P_NKEYS = 128
P_EXPERTS = P_NKEYS * P_NKEYS
P_TOPK = 16
P_CHUNK = 128
REL_BUCKETS = 32
REL_MAX_DIST = 128
N_BIAS_HEADS = A_HEADS + C_HEADS
Q_BLOCK = 128
N_BRANCH = 3
EPS = 1e-6
BRANCH_W = 512

SPLITS = (
    A_HEADS * 2 * A_DIM,
    A_HEADS * 2 * A_DIM,
    A_HEADS * 2 * A_DIM,
    B_GROUPS * B_GROUP_DIM,
    B_GROUPS * B_GROUP_DIM,
    C_HEADS * C_DIM,
    C_DIM,
    C_DIM,
    C_IDX_HEADS * C_IDX_DIM,
    C_IDX_DIM,
    C_IDX_HEADS,
    N_BRANCH * D_MODEL,
)
D_IN = sum(SPLITS)
SPLIT_POINTS = tuple(int(v) for v in np.cumsum(SPLITS)[:-1])

kernel_name = "hybrid_diffattn_gmlp_dsa_peer"


def rmsnorm(x, g):
    xf = x.astype(jnp.float32)
    y = xf * lax.rsqrt(jnp.mean(xf * xf, axis=-1, keepdims=True) + EPS)
    return (y * g.astype(jnp.float32)).astype(x.dtype)


def t5_bucket(dist):
    n = jnp.maximum(dist, 0)
    max_exact = REL_BUCKETS // 2
    nf = jnp.maximum(n, 1).astype(jnp.float32)
    large = max_exact + (jnp.log(nf / max_exact) / math.log(REL_MAX_DIST / max_exact)
                         * (REL_BUCKETS - max_exact)).astype(jnp.int32)
    large = jnp.minimum(large, REL_BUCKETS - 1)
    return jnp.where(n < max_exact, n, large)


def diff_attention(qa, ka, va, lam, rel_bias):
    bsz, seq = qa.shape[0], qa.shape[1]
    nb = seq // Q_BLOCK
    scale = A_DIM ** -0.5
    q_blocks = qa.reshape(bsz, nb, Q_BLOCK, A_HEADS, 2, A_DIM).swapaxes(0, 1)
    starts = jnp.arange(nb, dtype=jnp.int32) * Q_BLOCK
    k_pos = jnp.arange(seq, dtype=jnp.int32)

    def block(args):
        qb, start = args
        q_pos = start + jnp.arange(Q_BLOCK, dtype=jnp.int32)
        dist = q_pos[:, None] - k_pos[None, :]
        bias = rel_bias[t5_bucket(dist)][..., :A_HEADS]
        logits = jnp.einsum('bqhmd,bkhmd->bhmqk', qb, ka).astype(jnp.float32) * scale
        logits = logits + jnp.transpose(bias, (2, 0, 1))[None, :, None].astype(jnp.float32)
        logits = jnp.where(dist >= 0, logits, -jnp.inf)
        p = jax.nn.softmax(logits, axis=-1)
        p_diff = (p[:, :, 0] - lam * p[:, :, 1]).astype(va.dtype)
        return jnp.einsum('bhqk,bkhd->bqhd', p_diff, va)

    out = lax.map(block, (q_blocks, starts))
    return out.swapaxes(0, 1).reshape(bsz, seq, A_HEADS, 2 * A_DIM)


def spatial_gating(zu, zv, v_gain, w_s, b_s):
    bsz, seq = zu.shape[0], zu.shape[1]
    nc = seq // B_CHUNK
    u = jax.nn.gelu(zu, approximate=False)
    v = jax.nn.gelu(zv, approximate=False).reshape(bsz, seq, B_GROUPS, B_GROUP_DIM)
    v = rmsnorm(v, v_gain.reshape(B_GROUPS, B_GROUP_DIM))
    v = v.reshape(bsz, nc, B_CHUNK, B_GROUPS, B_GROUP_DIM)
    w = w_s * jnp.tril(jnp.ones((B_CHUNK, B_CHUNK), dtype=w_s.dtype))
    sv = jnp.einsum('gts,bcsgd->bctgd', w, v) + b_s.T[:, :, None]
    return u * sv.reshape(bsz, seq, B_GROUPS * B_GROUP_DIM)


def dsa_attention(cq, ck, cv, iq, ik, iw, rel_bias, k_sel):
    bsz, seq = cq.shape[0], cq.shape[1]
    nb = seq // Q_BLOCK
    scale = C_DIM ** -0.5
    to_blocks = lambda a: a.reshape((bsz, nb, Q_BLOCK) + a.shape[2:]).swapaxes(0, 1)
    starts = jnp.arange(nb, dtype=jnp.int32) * Q_BLOCK
    k_pos = jnp.arange(seq, dtype=jnp.int32)
    gather = jax.vmap(lambda a, i: a[i])

    def block(args):
        qb, iqb, iwb, start = args
        q_pos = start + jnp.arange(Q_BLOCK, dtype=jnp.int32)
        dist = q_pos[:, None] - k_pos[None, :]
        score = jnp.einsum('bqhk,bqh->bqk',
                           jax.nn.relu(jnp.einsum('bqhd,bkd->bqhk', iqb, ik)),
                           iwb).astype(jnp.float32)
        score = jnp.where(dist >= 0, score, -jnp.inf)
        _, sel = lax.top_k(score, k_sel)
        k_g = gather(ck, sel)
        v_g = gather(cv, sel)
        dsel = q_pos[None, :, None] - sel
        bias = rel_bias[t5_bucket(dsel)][..., A_HEADS:]
        logits = jnp.einsum('bqhd,bqkd->bqhk', qb, k_g).astype(jnp.float32) * scale
        logits = logits + jnp.swapaxes(bias, -1, -2).astype(jnp.float32)
        logits = jnp.where(dsel[:, :, None, :] >= 0, logits, -jnp.inf)
        p = jax.nn.softmax(logits, axis=-1).astype(cv.dtype)
        return jnp.einsum('bqhk,bqkd->bqhd', p, v_g)

    out = lax.map(block, (to_blocks(cq), to_blocks(iq), to_blocks(iw), starts))
    return out.swapaxes(0, 1).reshape(bsz, seq, C_HEADS * C_DIM)


def mixer_block(xn, rel_bias, w_in, a_q_gain, a_k_gain, a_lq1, a_lk1, a_lq2, a_lk2,
                a_subln_gain, lam_init, b_v_gain, b_w_s, b_b_s, c_q_gain, c_k_gain,
                w_br_a, w_br_b, w_br_c, w_out):
    bsz, seq, _ = xn.shape
    h = xn @ w_in
    (qa, ka, va, zu, zv, cq, ck, cv, iq, ik, iw, gts) = jnp.split(h, SPLIT_POINTS, axis=-1)
    qa = rmsnorm(qa.reshape(bsz, seq, A_HEADS, 2, A_DIM), a_q_gain)
    ka = rmsnorm(ka.reshape(bsz, seq, A_HEADS, 2, A_DIM), a_k_gain)
    va = va.reshape(bsz, seq, A_HEADS, 2 * A_DIM)
    lam = (jnp.exp(jnp.sum(a_lq1.astype(jnp.float32) * a_lk1.astype(jnp.float32)))
           - jnp.exp(jnp.sum(a_lq2.astype(jnp.float32) * a_lk2.astype(jnp.float32))) + lam_init)
    ya = diff_attention(qa, ka, va, lam, rel_bias)
    ya = (rmsnorm(ya, a_subln_gain) * (1.0 - lam_init)).reshape(bsz, seq, BRANCH_W)
    yb = spatial_gating(zu, zv, b_v_gain, b_w_s, b_b_s)
    k_sel = min(C_TOPK_MAX, seq // 4)
    cq = rmsnorm(cq.reshape(bsz, seq, C_HEADS, C_DIM), c_q_gain)
    ck = rmsnorm(ck, c_k_gain)
    iq = iq.reshape(bsz, seq, C_IDX_HEADS, C_IDX_DIM)
    yc = dsa_attention(cq, ck, cv, iq, ik, iw, rel_bias, k_sel)
    g = jax.nn.sigmoid(gts).reshape(bsz, seq, N_BRANCH, D_MODEL)
    merged = (g[:, :, 0] * (ya @ w_br_a) + g[:, :, 1] * (yb @ w_br_b)
              + g[:, :, 2] * (yc @ w_br_c))
    return merged @ w_out


def memory_xattn(xn, memn, wq, wkv, q_gain, k_gain, wo):
    bsz, seq, _ = xn.shape
    mlen = memn.shape[1]
    q = rmsnorm((xn @ wq).reshape(bsz, seq, M_HEADS, M_DIM), q_gain)
    k, v = jnp.split(memn @ wkv, 2, axis=-1)
    k = rmsnorm(k.reshape(bsz, mlen, M_HEADS, M_DIM), k_gain)
    v = v.reshape(bsz, mlen, M_HEADS, M_DIM)
    logits = jnp.einsum('bqhd,bkhd->bhqk', q, k).astype(jnp.float32) * (M_DIM ** -0.5)
    p = jax.nn.softmax(logits, axis=-1).astype(v.dtype)
    o = jnp.einsum('bhqk,bkhd->bqhd', p, v).reshape(bsz, seq, M_HEADS * M_DIM)
    return o @ wo


def peer(xn, wq, sk1, sk2, u_tab, v_tab):
    bsz, seq, dm = xn.shape
    q = (xn @ wq).reshape(bsz, seq, P_HEADS, 2, P_QDIM // 2)
    s1 = jnp.einsum('bshd,nd->bshn', q[..., 0, :], sk1).astype(jnp.float32)
    s2 = jnp.einsum('bshd,nd->bshn', q[..., 1, :], sk2).astype(jnp.float32)
    v1, i1 = lax.top_k(s1, P_TOPK)
    v2, i2 = lax.top_k(s2, P_TOPK)
    cand = (v1[..., :, None] + v2[..., None, :]).reshape(bsz, seq, P_HEADS, P_TOPK * P_TOPK)
    cs, ci = lax.top_k(cand, P_TOPK)
    e1 = jnp.take_along_axis(i1, ci // P_TOPK, axis=-1)
    e2 = jnp.take_along_axis(i2, ci % P_TOPK, axis=-1)
    expert = e1 * P_NKEYS + e2
    gate = jax.nn.softmax(cs, axis=-1).astype(xn.dtype)
    n_chunk = (bsz * seq) // P_CHUNK
    xc = xn.reshape(n_chunk, P_CHUNK, dm)
    ec = expert.reshape(n_chunk, P_CHUNK, P_HEADS, P_TOPK)
    gc = gate.reshape(n_chunk, P_CHUNK, P_HEADS, P_TOPK)

    def chunk(args):
        xt, et, gt = args
        act = jax.nn.gelu(jnp.einsum('td,thkd->thk', xt, u_tab[et]), approximate=False)
        return jnp.einsum('thk,thkd->td', gt * act, v_tab[et])

    return lax.map(chunk, (xc, ec, gc)).reshape(bsz, seq, dm)


def setup_inputs(seed: int = 0) -> dict:
    key = jax.random.key(seed)
    ks = iter(jax.random.split(key, 48))
    L, D = DEPTH, D_MODEL
    nrm = lambda shape, s: jax.random.normal(next(ks), shape, jnp.float32) * s
    gain = lambda shape: 1.0 + 0.05 * jax.random.normal(next(ks), shape, jnp.float32)
    return {
        "x": nrm((BATCH, SEQ, D), 1.0),
        "mem": nrm((BATCH, MEM_LEN, D), 1.0),
        "rel_bias": nrm((REL_BUCKETS, N_BIAS_HEADS), 0.5),
        "norm_mix": gain((L, D)),
        "w_in": nrm((L, D, D_IN), D ** -0.5),
        "a_q_gain": gain((L, A_DIM)),
        "a_k_gain": gain((L, A_DIM)),
        "a_lq1": nrm((L, A_DIM), 0.1),
        "a_lk1": nrm((L, A_DIM), 0.1),
        "a_lq2": nrm((L, A_DIM), 0.1),
        "a_lk2": nrm((L, A_DIM), 0.1),
        "a_subln_gain": gain((L, 2 * A_DIM)),
        "b_v_gain": gain((L, B_GROUPS * B_GROUP_DIM)),
        "b_w_s": nrm((L, B_GROUPS, B_CHUNK, B_CHUNK), 0.5 * B_CHUNK ** -0.5),
        "b_b_s": 1.0 + nrm((L, B_GROUPS, B_CHUNK), 0.1),
        "c_q_gain": gain((L, C_DIM)),
        "c_k_gain": gain((L, C_DIM)),
        "w_br_a": nrm((L, BRANCH_W, D), BRANCH_W ** -0.5),
        "w_br_b": nrm((L, BRANCH_W, D), BRANCH_W ** -0.5),
        "w_br_c": nrm((L, BRANCH_W, D), BRANCH_W ** -0.5),
        "w_mix_out": nrm((L, D, D), D ** -0.5),
        "norm_mem": gain((L, D)),
        "norm_memsrc": gain((L, D)),
        "m_wq": nrm((L, D, M_HEADS * M_DIM), D ** -0.5),
        "m_wkv": nrm((L, D, 2 * M_HEADS * M_DIM), D ** -0.5),
        "m_q_gain": gain((L, M_DIM)),
        "m_k_gain": gain((L, M_DIM)),
        "m_wo": nrm((L, M_HEADS * M_DIM, D), (M_HEADS * M_DIM) ** -0.5),
        "norm_peer": gain((L, D)),
        "p_wq": nrm((L, D, P_HEADS * P_QDIM), D ** -0.5),
        "p_subkey1": nrm((L, P_NKEYS, P_QDIM // 2), (P_QDIM // 2) ** -0.5),
        "p_subkey2": nrm((L, P_NKEYS, P_QDIM // 2), (P_QDIM // 2) ** -0.5),
        "p_u": nrm((L, P_EXPERTS, D), D ** -0.5),
        "p_v": nrm((L, P_EXPERTS, D), P_HEADS ** -0.5),
    }


def reference(x, mem, rel_bias, norm_mix, w_in, a_q_gain, a_k_gain, a_lq1, a_lk1, a_lq2,
              a_lk2, a_subln_gain, b_v_gain, b_w_s, b_b_s, c_q_gain, c_k_gain, w_br_a,
              w_br_b, w_br_c, w_mix_out, norm_mem, norm_memsrc, m_wq, m_wkv, m_q_gain,
              m_k_gain, m_wo, norm_peer, p_wq, p_subkey1, p_subkey2, p_u, p_v):
    for l in range(DEPTH):
        lam_init = 0.8 - 0.6 * math.exp(-0.3 * l)
        x = x + mixer_block(rmsnorm(x, norm_mix[l]), rel_bias, w_in[l], a_q_gain[l],
                            a_k_gain[l], a_lq1[l], a_lk1[l], a_lq2[l], a_lk2[l],
                            a_subln_gain[l], lam_init, b_v_gain[l], b_w_s[l], b_b_s[l],
                            c_q_gain[l], c_k_gain[l], w_br_a[l], w_br_b[l], w_br_c[l],
                            w_mix_out[l])
        x = x + memory_xattn(rmsnorm(x, norm_mem[l]), rmsnorm(mem, norm_memsrc[l]), m_wq[l],
                             m_wkv[l], m_q_gain[l], m_k_gain[l], m_wo[l])
        x = x + peer(rmsnorm(x, norm_peer[l]), p_wq[l], p_subkey1[l], p_subkey2[l],
                     p_u[l], p_v[l])
    return x
```

```python
import functools
import math

import jax
import jax.numpy as jnp
import numpy as np
from jax import lax
from jax.experimental import pallas as pl
from jax.experimental.pallas import tpu as pltpu

F32 = jnp.float32
BF16 = jnp.bfloat16
I32 = jnp.int32

EPS = 1e-6
NEG = -1e30
INT_MIN = -(2 ** 31)

A_HEADS = 4
A_DIM = 64
B_GROUPS = 4
B_GROUP_DIM = 128
B_CHUNK = 128
C_HEADS = 4
C_DIM = 128
C_IDX_HEADS = 8
C_IDX_DIM = 64
C_TOPK_MAX = 256
M_HEADS = 4
M_DIM = 128
P_HEADS = 8
P_QDIM = 256
P_NKEYS = 128
P_TOPK = 16
REL_BUCKETS = 32
REL_MAX_DIST = 128
LANES = 128

VMEM_LIMIT = 56 * 1024 * 1024


def _cparams(*sem):
    return pltpu.CompilerParams(dimension_semantics=sem, vmem_limit_bytes=VMEM_LIMIT)


def _gelu(x):
    return 0.5 * x * (1.0 + lax.erf(x * (1.0 / math.sqrt(2.0))))


def _group_norm(h, gain, group):
    parts = []
    for c in range(h.shape[-1] // LANES):
        hc = h[:, c * LANES:(c + 1) * LANES]
        sq = hc * hc
        if group == LANES:
            r = lax.rsqrt(jnp.sum(sq, -1, keepdims=True) * (1.0 / LANES) + EPS)
        else:
            lo = lax.broadcasted_iota(I32, hc.shape, 1) < 64
            s_lo = jnp.sum(jnp.where(lo, sq, 0.0), -1, keepdims=True)
            s_hi = jnp.sum(jnp.where(lo, 0.0, sq), -1, keepdims=True)
            r = jnp.where(lo, lax.rsqrt(s_lo * (1.0 / 64) + EPS), lax.rsqrt(s_hi * (1.0 / 64) + EPS))
        parts.append(hc * r)
    out = parts[0] if len(parts) == 1 else jnp.concatenate(parts, -1)
    return out * gain


def _epilogue(h, gain, mode):
    if mode == "none":
        return h
    if mode == "norm64":
        return _group_norm(h, gain, 64)
    if mode == "norm128":
        return _group_norm(h, gain, 128)
    if mode == "gelu":
        return _gelu(h)
    if mode == "gelu_norm128":
        return _group_norm(_gelu(h), gain, 128)
    if mode == "sigmoid":
        return jax.nn.sigmoid(h)
    raise ValueError(mode)


def _proj_kernel(x_ref, g_ref, w_ref, e_ref, o_ref, xn_ref, *, mode):
    @pl.when(pl.program_id(1) == 0)
    def _():
        x = x_ref[...]
        ms = jnp.mean(x * x, axis=-1, keepdims=True)
        xn_ref[...] = (x * lax.rsqrt(ms + EPS) * g_ref[...]).astype(BF16)

    h = jnp.dot(xn_ref[...], w_ref[...], preferred_element_type=F32)
    o_ref[...] = _epilogue(h, e_ref[...], mode).astype(o_ref.dtype)


def _proj(x, g, w, mode="none", gain=None, out_dtype=F32, tn=512):
    n, d = x.shape
    dout = w.shape[1]
    tm = min(1024, n)
    assert n % tm == 0 and dout % tn == 0, (n, dout, tn)
    if gain is None:
        gain = jnp.ones((dout,), F32)
    return pl.pallas_call(
        functools.partial(_proj_kernel, mode=mode),
        grid=(n // tm, dout // tn),
        in_specs=[
            pl.BlockSpec((tm, d), lambda i, j: (i, 0)),
            pl.BlockSpec((1, d), lambda i, j: (0, 0)),
            pl.BlockSpec((d, tn), lambda i, j: (0, j)),
            pl.BlockSpec((1, tn), lambda i, j: (0, j)),
        ],
        out_specs=pl.BlockSpec((tm, tn), lambda i, j: (i, j)),
        out_shape=jax.ShapeDtypeStruct((n, dout), out_dtype),
        scratch_shapes=[pltpu.VMEM((tm, d), BF16)],
        compiler_params=_cparams("parallel", "arbitrary"),
    )(x, g.reshape(1, d).astype(F32), w.astype(BF16), gain.reshape(1, dout).astype(F32))


def _t5_bucket(dist):
    n = jnp.maximum(dist, 0)
    max_exact = REL_BUCKETS // 2
    nf = jnp.maximum(n, 1).astype(F32)
    large = max_exact + (jnp.log(nf / max_exact) / math.log(REL_MAX_DIST / max_exact)
                         * (REL_BUCKETS - max_exact)).astype(I32)
    large = jnp.minimum(large, REL_BUCKETS - 1)
    return jnp.where(n < max_exact, n, large)


def _far_bucket_is_constant(t, seq):
    d = np.arange(t + 1, max(seq, t + 2), dtype=np.float64)
    max_exact = REL_BUCKETS // 2
    large = max_exact + np.floor(np.log(d / max_exact) / math.log(REL_MAX_DIST / max_exact)
                                 * (REL_BUCKETS - max_exact) + 1e-6).astype(np.int64)
    safe = max_exact + np.floor(np.log(d / max_exact) / math.log(REL_MAX_DIST / max_exact)
                                * (REL_BUCKETS - max_exact) - 1e-3).astype(np.int64)
    return bool(np.all(np.minimum(large, REL_BUCKETS - 1) == REL_BUCKETS - 1)
                and np.all(np.minimum(safe, REL_BUCKETS - 1) == REL_BUCKETS - 1))


def _bias_tiles(rel_bias, heads, t, seq):
    assert _far_bucket_is_constant(t, seq)
    tab = rel_bias[_t5_bucket(jnp.arange(2 * t, dtype=I32))][:, heads].astype(F32)
    i = jnp.arange(t, dtype=I32)[:, None]
    j = jnp.arange(t, dtype=I32)[None, :]
    d0 = i - j
    diag = jnp.where((d0 >= 0)[..., None], tab[jnp.maximum(d0, 0)], NEG)
    prev = tab[t + d0]
    far = rel_bias[REL_BUCKETS - 1, heads].astype(F32)
    nh = far.shape[0]
    return (jnp.transpose(diag, (2, 0, 1)), jnp.transpose(prev, (2, 0, 1)),
            jnp.broadcast_to(far[:, None, None], (nh, 1, LANES)))


def _diff_attn_kernel(lam_ref, q_ref, k_ref, v_ref, wd_ref, wp_ref, c_ref, g_ref, o_ref, *,
                      t, scale, lam_init):
    i = pl.program_id(2)
    q = q_ref[...]
    lo = lax.broadcasted_iota(I32, q.shape, 1) < A_DIM
    zero = jnp.zeros_like(q)
    qs = (jnp.where(lo, q, zero), jnp.where(lo, zero, q))
    far = c_ref[0:1, 0:1]

    def tile(start, bias, carry):
        k = k_ref[pl.ds(start, t), :]
        v = v_ref[pl.ds(start, t), :]
        out = []
        for m in range(2):
            mo, lo_, ao = carry[m]
            s = lax.dot_general(qs[m], k, (((1,), (1,)), ((), ())), preferred_element_type=F32)
            s = s * scale + bias
            mn = jnp.maximum(mo, jnp.max(s, -1, keepdims=True))
            a = jnp.exp(mo - mn)
            p = jnp.exp(s - mn)
            ln = a * lo_ + jnp.sum(p, -1, keepdims=True)
            an = a * ao + jnp.dot(p.astype(BF16), v, preferred_element_type=F32)
            out.append((mn, ln, an))
        return tuple(out)

    one = (jnp.full((t, 1), NEG, F32), jnp.zeros((t, 1), F32), jnp.zeros((t, LANES), F32))
    carry = lax.fori_loop(0, jnp.maximum(i - 1, 0),
                          lambda j, cr: tile(pl.multiple_of(j * t, t), far, cr), (one, one))
    jp = jnp.maximum(i - 1, 0)
    bias_p = jnp.where(i > 0, wp_ref[...], NEG)
    carry = tile(pl.multiple_of(jp * t, t), bias_p, carry)
    carry = tile(pl.multiple_of(i * t, t), wd_ref[...], carry)

    lv = lam_ref[...]
    lam = (jnp.exp(jnp.sum(lv[0:1] * lv[1:2], -1, keepdims=True))
           - jnp.exp(jnp.sum(lv[2:3] * lv[3:4], -1, keepdims=True)) + lam_init)
    o = carry[0][2] / carry[0][1] - lam * (carry[1][2] / carry[1][1])
    r = lax.rsqrt(jnp.mean(o * o, -1, keepdims=True) + EPS)
    o_ref[...] = ((o * r * g_ref[...]) * (1.0 - lam_init)).astype(o_ref.dtype)


def _diff_attn(qk, vv, lamv, bias, gain, lam_init, bsz, seq):
    t = min(256, seq)
    nq = seq // t
    wd, wp, far = bias
    return pl.pallas_call(
        functools.partial(_diff_attn_kernel, t=t, scale=A_DIM ** -0.5, lam_init=lam_init),
        grid=(bsz, A_HEADS, nq),
        in_specs=[
            pl.BlockSpec((8, LANES), lambda b, h, i: (0, 0)),
            pl.BlockSpec((None, t, LANES), lambda b, h, i: (b, i, h)),
            pl.BlockSpec((None, seq, LANES), lambda b, h, i: (b, 0, A_HEADS + h)),
            pl.BlockSpec((None, seq, LANES), lambda b, h, i: (b, 0, h)),
            pl.BlockSpec((None, t, t), lambda b, h, i: (h, 0, 0)),
            pl.BlockSpec((None, t, t), lambda b, h, i: (h, 0, 0)),
            pl.BlockSpec((None, 1, LANES), lambda b, h, i: (h, 0, 0)),
            pl.BlockSpec((1, LANES), lambda b, h, i: (0, 0)),
        ],
        out_specs=pl.BlockSpec((None, t, LANES), lambda b, h, i: (b, i, h)),
        out_shape=jax.ShapeDtypeStruct((bsz, seq, A_HEADS * LANES), BF16),
        compiler_params=_cparams("parallel", "parallel", "arbitrary"),
    )(lamv, qk, qk, vv, wd, wp, far, gain.reshape(1, LANES).astype(F32))


def _gmlp_kernel(u_ref, v_ref, w_ref, b_ref, o_ref, *, chunks):
    row = lax.broadcasted_iota(I32, (B_CHUNK, B_CHUNK), 0)
    col = lax.broadcasted_iota(I32, (B_CHUNK, B_CHUNK), 1)
    ws = [jnp.where(row >= col, w_ref[g], 0.0).astype(BF16) for g in range(B_GROUPS)]
    for c in range(chunks):
        rs = slice(c * B_CHUNK, (c + 1) * B_CHUNK)
        for g in range(B_GROUPS):
            cs = slice(g * B_GROUP_DIM, (g + 1) * B_GROUP_DIM)
            sv = jnp.dot(ws[g], v_ref[rs, cs], preferred_element_type=F32) + b_ref[:, cs]
            o_ref[rs, cs] = (u_ref[rs, cs] * sv).astype(o_ref.dtype)


def _gmlp(u, v, w_s, b_s):
    n, width = u.shape
    chunks = 4
    tm = chunks * B_CHUNK
    bfull = jnp.repeat(b_s.T.astype(F32), B_GROUP_DIM, axis=1)
    return pl.pallas_call(
        functools.partial(_gmlp_kernel, chunks=chunks),
        grid=(n // tm,),
        in_specs=[
            pl.BlockSpec((tm, width), lambda i: (i, 0)),
            pl.BlockSpec((tm, width), lambda i: (i, 0)),
            pl.BlockSpec((B_GROUPS, B_CHUNK, B_CHUNK), lambda i: (0, 0, 0)),
            pl.BlockSpec((B_CHUNK, width), lambda i: (0, 0)),
        ],
        out_specs=pl.BlockSpec((tm, width), lambda i: (i, 0)),
        out_shape=jax.ShapeDtypeStruct((n, width), BF16),
        compiler_params=_cparams("parallel"),
    )(u, v, w_s.astype(F32), bfull)


def _dsa_kernel(cq_ref, ck_ref, cv_ref, iq_ref, ikk_ref, iwq_ref, wd_ref, wp_ref, c_ref, o_ref,
                key_ref, *, t, seq, k_sel, scale):
    i = pl.program_id(1)
    nt = seq // t
    qpos = i * t + lax.broadcasted_iota(I32, (t, seq), 0)
    kpos = lax.broadcasted_iota(I32, (t, seq), 1)
    valid = kpos <= qpos

    ikk = ikk_ref[...].astype(BF16)
    iw = iwq_ref[...]
    score = jnp.zeros((t, seq), F32)
    for pair in range(C_IDX_HEADS // 2):
        qp = iq_ref[:, pair * LANES:(pair + 1) * LANES].astype(BF16)
        lo = lax.broadcasted_iota(I32, qp.shape, 1) < C_IDX_DIM
        zero = jnp.zeros_like(qp)
        for half in range(2):
            h = 2 * pair + half
            qh = jnp.where(lo, qp, zero) if half == 0 else jnp.where(lo, zero, qp)
            d = lax.dot_general(qh, ikk, (((1,), (1,)), ((), ())), preferred_element_type=F32)
            score = score + jnp.maximum(d, 0.0) * iw[:, h:h + 1]

    bits = pltpu.bitcast(score + 0.0, I32)
    skey = bits ^ ((bits >> 31) & 0x7FFFFFFF)
    key_ref[...] = jnp.where(valid, skey, INT_MIN)

    def count(mask):
        return jnp.sum(jnp.where(mask, 1.0, 0.0), axis=-1, keepdims=True)

    def vbody(it, p_u):
        cand = p_u | (jnp.int32(1) << (31 - it))
        cnt = count(key_ref[...] >= (cand ^ INT_MIN))
        return jnp.where(cnt >= k_sel, cand, p_u)

    p_u = lax.fori_loop(0, 32, vbody, jnp.zeros((t, 1), I32))
    thr = p_u ^ INT_MIN
    keys = key_ref[...]
    gt = keys > thr
    eq = (keys == thr) & valid
    need = k_sel - count(gt)

    nbits = max(1, (seq - 1).bit_length())

    def ibody(it, m):
        cand = m | (jnp.int32(1) << (nbits - 1 - it))
        cnt = count(eq & (kpos < cand))
        return jnp.where(cnt < need, cand, m)

    m = lax.fori_loop(0, nbits, ibody, jnp.zeros((t, 1), I32))
    sel = valid & (gt | (eq & (kpos <= m)))

    ck = ck_ref[...]
    cv = cv_ref[...]
    for h in range(C_HEADS):
        far = c_ref[h, 0:1, 0:1]
        tiles = []
        for jt in range(nt):
            tiles.append(jnp.where(i == jt, wd_ref[h], jnp.where(i == jt + 1, wp_ref[h], far)))
        bias = tiles[0] if nt == 1 else jnp.concatenate(tiles, -1)
        s = lax.dot_general(cq_ref[:, h * C_DIM:(h + 1) * C_DIM], ck, (((1,), (1,)), ((), ())),
                            preferred_element_type=F32)
        s = jnp.where(sel, s * scale + bias, NEG)
        p = jnp.exp(s - jnp.max(s, -1, keepdims=True))
        p = p / jnp.sum(p, -1, keepdims=True)
        o_ref[:, h * C_DIM:(h + 1) * C_DIM] = jnp.dot(
            p.astype(BF16), cv, preferred_element_type=F32).astype(o_ref.dtype)


def _dsa(cqk, vv, idx, bias, bsz, seq, k_sel):
    t = LANES
    wd, wp, far = bias
    nq = seq // t
    return pl.pallas_call(
        functools.partial(_dsa_kernel, t=t, seq=seq, k_sel=k_sel, scale=C_DIM ** -0.5),
        grid=(bsz, nq),
        in_specs=[
            pl.BlockSpec((None, t, C_HEADS * C_DIM), lambda b, i: (b, i, 0)),
            pl.BlockSpec((None, seq, C_DIM), lambda b, i: (b, 0, C_HEADS)),
            pl.BlockSpec((None, seq, C_DIM), lambda b, i: (b, 0, C_HEADS)),
            pl.BlockSpec((None, t, C_IDX_HEADS * C_IDX_DIM), lambda b, i: (b, i, 0)),
            pl.BlockSpec((None, seq, LANES), lambda b, i: (b, 0, 4)),
            pl.BlockSpec((None, t, LANES), lambda b, i: (b, i, 5)),
            pl.BlockSpec((C_HEADS, t, t), lambda b, i: (0, 0, 0)),
            pl.BlockSpec((C_HEADS, t, t), lambda b, i: (0, 0, 0)),
            pl.BlockSpec((C_HEADS, 1, LANES), lambda b, i: (0, 0, 0)),
        ],
        out_specs=pl.BlockSpec((None, t, C_HEADS * C_DIM), lambda b, i: (b, i, 0)),
        out_shape=jax.ShapeDtypeStruct((bsz, seq, C_HEADS * C_DIM), BF16),
        scratch_shapes=[pltpu.VMEM((t, seq), I32)],
        compiler_params=_cparams("parallel", "arbitrary"),
    )(cqk, cqk, vv, idx, idx, idx, wd, wp, far)


def _merge_kernel(x_ref, ya_ref, yb_ref, yc_ref, g_ref, wa_ref, wb_ref, wc_ref, wo_ref, o_ref, *, d):
    merged = (g_ref[:, 0:d] * jnp.dot(ya_ref[...], wa_ref[...], preferred_element_type=F32)
              + g_ref[:, d:2 * d] * jnp.dot(yb_ref[...], wb_ref[...], preferred_element_type=F32)
              + g_ref[:, 2 * d:3 * d] * jnp.dot(yc_ref[...], wc_ref[...], preferred_element_type=F32))
    o_ref[...] = x_ref[...] + jnp.dot(merged.astype(BF16), wo_ref[...], preferred_element_type=F32)


def _merge(x, ya, yb, yc, gates, wa, wb, wc, wo):
    n, d = x.shape
    bw = ya.shape[1]
    tm = min(512, n)
    row = lambda w: pl.BlockSpec((tm, w), lambda i: (i, 0))
    full = lambda a, b: pl.BlockSpec((a, b), lambda i: (0, 0))
    return pl.pallas_call(
        functools.partial(_merge_kernel, d=d),
        grid=(n // tm,),
        in_specs=[row(d), row(bw), row(bw), row(bw), row(3 * d),
                  full(bw, d), full(bw, d), full(bw, d), full(d, d)],
        out_specs=row(d),
        out_shape=jax.ShapeDtypeStruct((n, d), F32),
        compiler_params=_cparams("parallel"),
    )(x, ya, yb, yc, gates, wa.astype(BF16), wb.astype(BF16), wc.astype(BF16), wo.astype(BF16))


def _mem_attn_kernel(x_ref, q_ref, k_ref, v_ref, wo_ref, o_ref, *, scale):
    outs = []
    for h in range(M_HEADS):
        cs = slice(h * M_DIM, (h + 1) * M_DIM)
        s = lax.dot_general(q_ref[:, cs], k_ref[:, cs], (((1,), (1,)), ((), ())),
                            preferred_element_type=F32) * scale
        p = jnp.exp(s - jnp.max(s, -1, keepdims=True))
        p = p / jnp.sum(p, -1, keepdims=True)
        outs.append(jnp.dot(p.astype(BF16), v_ref[:, cs], preferred_element_type=F32).astype(BF16))
    o = jnp.concatenate(outs, -1)
    o_ref[...] = x_ref[...] + jnp.dot(o, wo_ref[...], preferred_element_type=F32)


def _mem_attn(x, q, k, v, wo, bsz, seq, mlen):
    d = x.shape[-1]
    w = M_HEADS * M_DIM
    t = min(512, seq)
    return pl.pallas_call(
        functools.partial(_mem_attn_kernel, scale=M_DIM ** -0.5),
        grid=(bsz, seq // t),
        in_specs=[
            pl.BlockSpec((None, t, d), lambda b, i: (b, i, 0)),
            pl.BlockSpec((None, t, w), lambda b, i: (b, i, 0)),
            pl.BlockSpec((None, mlen, w), lambda b, i: (b, 0, 0)),
            pl.BlockSpec((None, mlen, w), lambda b, i: (b, 0, 0)),
            pl.BlockSpec((w, d), lambda b, i: (0, 0)),
        ],
        out_specs=pl.BlockSpec((None, t, d), lambda b, i: (b, i, 0)),
        out_shape=jax.ShapeDtypeStruct((bsz, seq, d), F32),
        compiler_params=_cparams("parallel", "arbitrary"),
    )(x, q, k, v, wo.astype(BF16))


def _topk_rows(arr, k):
    nrow = arr.shape[0]
    rid = lax.broadcasted_iota(I32, arr.shape, 0)
    vals, idxs = [], []
    for _ in range(k):
        m = jnp.max(arr, axis=0, keepdims=True)
        am = jnp.min(jnp.where(arr == m, rid, nrow), axis=0, keepdims=True)
        vals.append(m)
        idxs.append(am)
        arr = jnp.where(rid == am, -jnp.inf, arr)
    return jnp.concatenate(vals, 0), jnp.concatenate(idxs, 0)


def _pick_rows(table, sel):
    out = jnp.zeros(sel.shape, table.dtype)
    for r in range(table.shape[0]):
        out = jnp.where(sel == r, table[r:r + 1, :], out)
    return out


def _peer_topk_kernel(q_ref, k1_ref, k2_ref, ids_ref, gate_ref):
    q = q_ref[...].astype(BF16)
    half = P_QDIM // 2
    dn = (((1,), (1,)), ((), ()))
    s1 = lax.dot_general(k1_ref[...], q[:, :half], dn, preferred_element_type=F32)
    s2 = lax.dot_general(k2_ref[...], q[:, half:], dn, preferred_element_type=F32)
    v1, i1 = _topk_rows(s1, P_TOPK)
    v2, i2 = _topk_rows(s2, P_TOPK)
    cand = jnp.concatenate([v1[a:a + 1, :] + v2 for a in range(P_TOPK)], 0)
    cs, ci = _topk_rows(cand, P_TOPK)
    e1 = _pick_rows(i1, ci >> 4)
    e2 = _pick_rows(i2, ci & (P_TOPK - 1))
    ids_ref[...] = e1 * P_NKEYS + e2
    p = jnp.exp(cs - jnp.max(cs, axis=0, keepdims=True))
    gate_ref[...] = p / jnp.sum(p, axis=0, keepdims=True)


def _peer_topk(q, sk1, sk2):
    n = q.shape[0]
    t = LANES
    assert P_TOPK == 16
    return pl.pallas_call(
        _peer_topk_kernel,
        grid=(n // t, P_HEADS),
        in_specs=[
            pl.BlockSpec((t, P_QDIM), lambda i, h: (i, h)),
            pl.BlockSpec((P_NKEYS, P_QDIM // 2), lambda i, h: (0, 0)),
            pl.BlockSpec((P_NKEYS, P_QDIM // 2), lambda i, h: (0, 0)),
        ],
        out_specs=[
            pl.BlockSpec((P_TOPK, t), lambda i, h: (h, i)),
            pl.BlockSpec((P_TOPK, t), lambda i, h: (h, i)),
        ],
        out_shape=[
            jax.ShapeDtypeStruct((P_HEADS * P_TOPK, n), I32),
            jax.ShapeDtypeStruct((P_HEADS * P_TOPK, n), F32),
        ],
        compiler_params=_cparams("parallel", "arbitrary"),
    )(q, sk1.astype(BF16), sk2.astype(BF16))


PEER_TOKENS_PER_STEP = 128
PEER_TOKENS_PER_SLOT = 8


def _peer_expert_kernel(ids_hbm, x_ref, g_ref, gate_ref, u_hbm, v_hbm, o_ref,
                        ids_smem, xn_ref, ubuf, vbuf, sem_ids, sem_u, sem_v, *, tt, tb, npair):
    i = pl.program_id(0)
    rows = tb * npair
    nsub = tt // tb

    ids_cp = pltpu.make_async_copy(ids_hbm.at[i], ids_smem, sem_ids)
    ids_cp.start()
    x = x_ref[...]
    ms = jnp.mean(x * x, axis=-1, keepdims=True)
    xn_ref[...] = x * lax.rsqrt(ms + EPS) * g_ref[...]
    ids_cp.wait()

    def issue(sb, slot):
        def body(r, carry):
            e = ids_smem[sb * rows + r]
            pltpu.make_async_copy(u_hbm.at[pl.ds(e, 1)], ubuf.at[slot, pl.ds(r, 1)], sem_u.at[slot]).start()
            pltpu.make_async_copy(v_hbm.at[pl.ds(e, 1)], vbuf.at[slot, pl.ds(r, 1)], sem_v.at[slot]).start()
            return carry
        lax.fori_loop(0, rows, body, 0, unroll=8)

    def wait(slot):
        pltpu.make_async_copy(u_hbm.at[pl.ds(0, rows)], ubuf.at[slot], sem_u.at[slot]).wait()
        pltpu.make_async_copy(v_hbm.at[pl.ds(0, rows)], vbuf.at[slot], sem_v.at[slot]).wait()

    lane = lax.broadcasted_iota(I32, (npair, tt), 1)

    def compute(sb, slot):
        def body(j, carry):
            tok = sb * tb + j
            r0 = pl.multiple_of(j * npair, npair)
            xrow = xn_ref[pl.ds(tok, 1), :]
            u = ubuf[slot, pl.ds(r0, npair), :]
            hdot = jnp.sum(u * xrow, axis=-1, keepdims=True)
            gcol = jnp.sum(jnp.where(lane == tok, gate_ref[...], 0.0), axis=-1, keepdims=True)
            coef = gcol * _gelu(hdot)
            v = vbuf[slot, pl.ds(r0, npair), :]
            out = jnp.sum(v * coef, axis=0, keepdims=True)
            o_ref[pl.ds(tok, 1), :] = x_ref[pl.ds(tok, 1), :] + out
            return carry
        lax.fori_loop(0, tb, body, 0)

    issue(0, 0)

    def sub(sb, carry):
        slot = sb & 1

        @pl.when(sb + 1 < nsub)
        def _():
            issue(sb + 1, 1 - slot)

        wait(slot)
        compute(sb, slot)
        return carry

    lax.fori_loop(0, nsub, sub, 0)


def _peer_expert(x, g, ids_t, gates_t, u_tab, v_tab):
    n, d = x.shape
    npair = ids_t.shape[0]
    tt, tb = PEER_TOKENS_PER_STEP, PEER_TOKENS_PER_SLOT
    assert n % tt == 0 and tt % tb == 0
    ids = ids_t.T.reshape(n // tt, tt * npair)
    return pl.pallas_call(
        functools.partial(_peer_expert_kernel, tt=tt, tb=tb, npair=npair),
        grid=(n // tt,),
        in_specs=[
            pl.BlockSpec(memory_space=pl.ANY),
            pl.BlockSpec((tt, d), lambda i: (i, 0)),
            pl.BlockSpec((1, d), lambda i: (0, 0)),
            pl.BlockSpec((npair, tt), lambda i: (0, i)),
            pl.BlockSpec(memory_space=pl.ANY),
            pl.BlockSpec(memory_space=pl.ANY),
        ],
        out_specs=pl.BlockSpec((tt, d), lambda i: (i, 0)),
        out_shape=jax.ShapeDtypeStruct((n, d), F32),
        scratch_shapes=[
            pltpu.SMEM((tt * npair,), I32),
            pltpu.VMEM((tt, d), F32),
            pltpu.VMEM((2, tb * npair, d), F32),
            pltpu.VMEM((2, tb * npair, d), F32),
            pltpu.SemaphoreType.DMA,
            pltpu.SemaphoreType.DMA((2,)),
            pltpu.SemaphoreType.DMA((2,)),
        ],
        compiler_params=_cparams("arbitrary"),
    )(ids, x, g.reshape(1, d).astype(F32), gates_t, u_tab, v_tab)


def _tile_gain(g, reps):
    return jnp.tile(g.astype(F32), reps)


def _layer(l, x, memn_kv, bias_a, bias_c, p):
    bsz, seq, d = x.shape
    n = bsz * seq
    lam_init = 0.8 - 0.6 * math.exp(-0.3 * l)
    x2 = x.reshape(n, d)
    w_in = p["w_in"][l]
    cols = np.cumsum([0, 512, 512, 512, 512, 512, 512, 128, 128, 512, 64, 8, 3 * d])
    seg = lambda a, b: w_in[:, cols[a]:cols[b]]
    gmix = p["norm_mix"][l]

    qk = _proj(x2, gmix, seg(0, 2), "norm64",
               jnp.concatenate([_tile_gain(p["a_q_gain"][l], 8), _tile_gain(p["a_k_gain"][l], 8)]), BF16)
    vv = _proj(x2, gmix, jnp.concatenate([seg(2, 3), seg(7, 8)], 1), "none", None, BF16, tn=640)
    u = _proj(x2, gmix, seg(3, 4), "gelu", None, F32)
    v = _proj(x2, gmix, seg(4, 5), "gelu_norm128", p["b_v_gain"][l], BF16)
    cqk = _proj(x2, gmix, seg(5, 7), "norm128",
                jnp.concatenate([_tile_gain(p["c_q_gain"][l], 4), p["c_k_gain"][l].astype(F32)]), BF16, tn=640)
    w_idx = jnp.concatenate([seg(8, 9), seg(9, 10), seg(9, 10), seg(10, 11),
                             jnp.zeros((d, 120), w_in.dtype)], 1)
    idx = _proj(x2, gmix, w_idx, "none", None, F32, tn=768)
    gates = _proj(x2, gmix, seg(11, 12), "sigmoid", None, F32)

    lamv = jnp.zeros((8, LANES), F32)
    for r, name in enumerate(("a_lq1", "a_lk1", "a_lq2", "a_lk2")):
        lamv = lamv.at[r, :A_DIM].set(p[name][l].astype(F32))
    ya = _diff_attn(qk.reshape(bsz, seq, -1), vv.reshape(bsz, seq, -1), lamv, bias_a,
                    p["a_subln_gain"][l], lam_init, bsz, seq)
    yb = _gmlp(u, v, p["b_w_s"][l], p["b_b_s"][l])
    k_sel = min(C_TOPK_MAX, seq // 4)
    yc = _dsa(cqk.reshape(bsz, seq, -1), vv.reshape(bsz, seq, -1), idx.reshape(bsz, seq, -1),
              bias_c, bsz, seq, k_sel)
    x2 = _merge(x2, ya.reshape(n, -1), yb, yc.reshape(n, -1), gates,
                p["w_br_a"][l], p["w_br_b"][l], p["w_br_c"][l], p["w_mix_out"][l])

    mk, mv = memn_kv
    mq = _proj(x2, p["norm_mem"][l], p["m_wq"][l], "norm128", _tile_gain(p["m_q_gain"][l], M_HEADS), BF16)
    x3 = _mem_attn(x2.reshape(bsz, seq, d), mq.reshape(bsz, seq, -1), mk, mv, p["m_wo"][l],
                   bsz, seq, mk.shape[1])
    x2 = x3.reshape(n, d)

    pq = _proj(x2, p["norm_peer"][l], p["p_wq"][l], "none", None, F32)
    ids_t, gates_t = _peer_topk(pq, p["p_subkey1"][l], p["p_subkey2"][l])
    x2 = _peer_expert(x2, p["norm_peer"][l], ids_t, gates_t, p["p_u"][l], p["p_v"][l])
    return x2.reshape(bsz, seq, d)


def kernel(x, mem, rel_bias, norm_mix, w_in, a_q_gain, a_k_gain, a_lq1, a_lk1, a_lq2, a_lk2, a_subln_gain, b_v_gain, b_w_s, b_b_s, c_q_gain, c_k_gain, w_br_a, w_br_b, w_br_c, w_mix_out, norm_mem, norm_memsrc, m_wq, m_wkv, m_q_gain, m_k_gain, m_wo, norm_peer, p_wq, p_subkey1, p_subkey2, p_u, p_v):
    p = dict(norm_mix=norm_mix, w_in=w_in, a_q_gain=a_q_gain, a_k_gain=a_k_gain, a_lq1=a_lq1,
             a_lk1=a_lk1, a_lq2=a_lq2, a_lk2=a_lk2, a_subln_gain=a_subln_gain, b_v_gain=b_v_gain,
             b_w_s=b_w_s, b_b_s=b_b_s, c_q_gain=c_q_gain, c_k_gain=c_k_gain, w_br_a=w_br_a,
             w_br_b=w_br_b, w_br_c=w_br_c, w_mix_out=w_mix_out, norm_mem=norm_mem, m_wq=m_wq,
             m_q_gain=m_q_gain, m_wo=m_wo, norm_peer=norm_peer, p_wq=p_wq, p_subkey1=p_subkey1,
             p_subkey2=p_subkey2, p_u=p_u, p_v=p_v)
    bsz, seq, d = x.shape
    mlen = mem.shape[1]
    depth = w_in.shape[0]
    bias_a = _bias_tiles(rel_bias, slice(0, A_HEADS), min(256, seq), seq)
    bias_c = _bias_tiles(rel_bias, slice(A_HEADS, A_HEADS + C_HEADS), LANES, seq)
    mem2 = mem.reshape(bsz * mlen, d)
    w = M_HEADS * M_DIM
    for l in range(depth):
        mk = _proj(mem2, norm_memsrc[l], m_wkv[l][:, :w], "norm128", _tile_gain(m_k_gain[l], M_HEADS), BF16)
        mv = _proj(mem2, norm_memsrc[l], m_wkv[l][:, w:], "none", None, BF16)
        x = _layer(l, x, (mk.reshape(bsz, mlen, w), mv.reshape(bsz, mlen, w)), bias_a, bias_c, p)
    return x
```

```python
import functools
import math

import jax
import jax.numpy as jnp
import numpy as np
from jax import lax
from jax.experimental import pallas as pl
from jax.experimental.pallas import tpu as pltpu

F32 = jnp.float32
BF16 = jnp.bfloat16
I32 = jnp.int32

EPS = 1e-6
NEG = -1e30
INT_MIN = -(2 ** 31)

A_HEADS = 4
A_DIM = 64
B_GROUPS = 4
B_GROUP_DIM = 128
B_CHUNK = 128
C_HEADS = 4
C_DIM = 128
C_IDX_HEADS = 8
C_IDX_DIM = 64
C_TOPK_MAX = 256
M_HEADS = 4
M_DIM = 128
P_HEADS = 8
P_QDIM = 256
P_NKEYS = 128
P_TOPK = 16
REL_BUCKETS = 32
REL_MAX_DIST = 128
LANES = 128

VMEM_LIMIT = 56 * 1024 * 1024


def _cparams(*sem):
    return pltpu.CompilerParams(dimension_semantics=sem, vmem_limit_bytes=VMEM_LIMIT)


def _gelu(x):
    return 0.5 * x * (1.0 + lax.erf(x * (1.0 / math.sqrt(2.0))))


def _group_norm(h, gain, group):
    parts = []
    for c in range(h.shape[-1] // LANES):
        hc = h[:, c * LANES:(c + 1) * LANES]
        sq = hc * hc
        if group == LANES:
            r = lax.rsqrt(jnp.sum(sq, -1, keepdims=True) * (1.0 / LANES) + EPS)
        else:
            lo = lax.broadcasted_iota(I32, hc.shape, 1) < 64
            s_lo = jnp.sum(jnp.where(lo, sq, 0.0), -1, keepdims=True)
            s_hi = jnp.sum(jnp.where(lo, 0.0, sq), -1, keepdims=True)
            r = jnp.where(lo, lax.rsqrt(s_lo * (1.0 / 64) + EPS), lax.rsqrt(s_hi * (1.0 / 64) + EPS))
        parts.append(hc * r)
    out = parts[0] if len(parts) == 1 else jnp.concatenate(parts, -1)
    return out * gain


def _epilogue(h, gain, mode):
    if mode == "none":
        return h
    if mode == "norm64":
        return _group_norm(h, gain, 64)
    if mode == "norm128":
        return _group_norm(h, gain, 128)
    if mode == "gelu":
        return _gelu(h)
    if mode == "gelu_norm128":
        return _group_norm(_gelu(h), gain, 128)
    if mode == "sigmoid":
        return jax.nn.sigmoid(h)
    raise ValueError(mode)


def _proj_kernel(x_ref, g_ref, w_ref, e_ref, o_ref, xn_ref, *, mode):
    @pl.when(pl.program_id(1) == 0)
    def _():
        x = x_ref[...]
        ms = jnp.mean(x * x, axis=-1, keepdims=True)
        xn_ref[...] = (x * lax.rsqrt(ms + EPS) * g_ref[...]).astype(BF16)

    h = jnp.dot(xn_ref[...], w_ref[...], preferred_element_type=F32)
    o_ref[...] = _epilogue(h, e_ref[...], mode).astype(o_ref.dtype)


def _proj(x, g, w, mode="none", gain=None, out_dtype=F32, tn=512):
    n, d = x.shape
    dout = w.shape[1]
    tm = min(1024, n)
    assert n % tm == 0 and dout % tn == 0, (n, dout, tn)
    if gain is None:
        gain = jnp.ones((dout,), F32)
    return pl.pallas_call(
        functools.partial(_proj_kernel, mode=mode),
        grid=(n // tm, dout // tn),
        in_specs=[
            pl.BlockSpec((tm, d), lambda i, j: (i, 0)),
            pl.BlockSpec((1, d), lambda i, j: (0, 0)),
            pl.BlockSpec((d, tn), lambda i, j: (0, j)),
            pl.BlockSpec((1, tn), lambda i, j: (0, j)),
        ],
        out_specs=pl.BlockSpec((tm, tn), lambda i, j: (i, j)),
        out_shape=jax.ShapeDtypeStruct((n, dout), out_dtype),
        scratch_shapes=[pltpu.VMEM((tm, d), BF16)],
        compiler_params=_cparams("parallel", "arbitrary"),
    )(x, g.reshape(1, d).astype(F32), w.astype(BF16), gain.reshape(1, dout).astype(F32))


def _t5_bucket(dist):
    n = jnp.maximum(dist, 0)
    max_exact = REL_BUCKETS // 2
    nf = jnp.maximum(n, 1).astype(F32)
    large = max_exact + (jnp.log(nf / max_exact) / math.log(REL_MAX_DIST / max_exact)
                         * (REL_BUCKETS - max_exact)).astype(I32)
    large = jnp.minimum(large, REL_BUCKETS - 1)
    return jnp.where(n < max_exact, n, large)


def _far_bucket_is_constant(t, seq):
    d = np.arange(t + 1, max(seq, t + 2), dtype=np.float64)
    max_exact = REL_BUCKETS // 2
    large = max_exact + np.floor(np.log(d / max_exact) / math.log(REL_MAX_DIST / max_exact)
                                 * (REL_BUCKETS - max_exact) + 1e-6).astype(np.int64)
    safe = max_exact + np.floor(np.log(d / max_exact) / math.log(REL_MAX_DIST / max_exact)
                                * (REL_BUCKETS - max_exact) - 1e-3).astype(np.int64)
    return bool(np.all(np.minimum(large, REL_BUCKETS - 1) == REL_BUCKETS - 1)
                and np.all(np.minimum(safe, REL_BUCKETS - 1) == REL_BUCKETS - 1))


def _bias_tiles(rel_bias, heads, t, seq):
    assert _far_bucket_is_constant(t, seq)
    tab = rel_bias[_t5_bucket(jnp.arange(2 * t, dtype=I32))][:, heads].astype(F32)
    i = jnp.arange(t, dtype=I32)[:, None]
    j = jnp.arange(t, dtype=I32)[None, :]
    d0 = i - j
    diag = jnp.where((d0 >= 0)[..., None], tab[jnp.maximum(d0, 0)], NEG)
    prev = tab[t + d0]
    far = rel_bias[REL_BUCKETS - 1, heads].astype(F32)
    nh = far.shape[0]
    return (jnp.transpose(diag, (2, 0, 1)), jnp.transpose(prev, (2, 0, 1)),
            jnp.broadcast_to(far[:, None, None], (nh, 1, LANES)))


def _diff_attn_kernel(lam_ref, q_ref, k_ref, v_ref, wd_ref, wp_ref, c_ref, g_ref, o_ref, *,
                      t, scale, lam_init):
    i = pl.program_id(2)
    q = q_ref[...]
    lo = lax.broadcasted_iota(I32, q.shape, 1) < A_DIM
    zero = jnp.zeros_like(q)
    qs = (jnp.where(lo, q, zero), jnp.where(lo, zero, q))
    far = c_ref[0:1, 0:1]

    def tile(start, bias, carry):
        k = k_ref[pl.ds(start, t), :]
        v = v_ref[pl.ds(start, t), :]
        out = []
        for m in range(2):
            mo, lo_, ao = carry[m]
            s = lax.dot_general(qs[m], k, (((1,), (1,)), ((), ())), preferred_element_type=F32)
            s = s * scale + bias
            mn = jnp.maximum(mo, jnp.max(s, -1, keepdims=True))
            a = jnp.exp(mo - mn)
            p = jnp.exp(s - mn)
            ln = a * lo_ + jnp.sum(p, -1, keepdims=True)
            an = a * ao + jnp.dot(p.astype(BF16), v, preferred_element_type=F32)
            out.append((mn, ln, an))
        return tuple(out)

    one = (jnp.full((t, 1), NEG, F32), jnp.zeros((t, 1), F32), jnp.zeros((t, LANES), F32))
    carry = lax.fori_loop(0, jnp.maximum(i - 1, 0),
                          lambda j, cr: tile(pl.multiple_of(j * t, t), far, cr), (one, one))
    jp = jnp.maximum(i - 1, 0)
    bias_p = jnp.where(i > 0, wp_ref[...], NEG)
    carry = tile(pl.multiple_of(jp * t, t), bias_p, carry)
    carry = tile(pl.multiple_of(i * t, t), wd_ref[...], carry)

    lv = lam_ref[...]
    lam = (jnp.exp(jnp.sum(lv[0:1] * lv[1:2], -1, keepdims=True))
           - jnp.exp(jnp.sum(lv[2:3] * lv[3:4], -1, keepdims=True)) + lam_init)
    o = carry[0][2] / carry[0][1] - lam * (carry[1][2] / carry[1][1])
    r = lax.rsqrt(jnp.mean(o * o, -1, keepdims=True) + EPS)
    o_ref[...] = ((o * r * g_ref[...]) * (1.0 - lam_init)).astype(o_ref.dtype)


def _diff_attn(qk, vv, lamv, bias, gain, lam_init, bsz, seq):
    t = min(256, seq)
    nq = seq // t
    wd, wp, far = bias
    return pl.pallas_call(
        functools.partial(_diff_attn_kernel, t=t, scale=A_DIM ** -0.5, lam_init=lam_init),
        grid=(bsz, A_HEADS, nq),
        in_specs=[
            pl.BlockSpec((8, LANES), lambda b, h, i: (0, 0)),
            pl.BlockSpec((None, t, LANES), lambda b, h, i: (b, i, h)),
            pl.BlockSpec((None, seq, LANES), lambda b, h, i: (b, 0, A_HEADS + h)),
            pl.BlockSpec((None, seq, LANES), lambda b, h, i: (b, 0, h)),
            pl.BlockSpec((None, t, t), lambda b, h, i: (h, 0, 0)),
            pl.BlockSpec((None, t, t), lambda b, h, i: (h, 0, 0)),
            pl.BlockSpec((None, 1, LANES), lambda b, h, i: (h, 0, 0)),
            pl.BlockSpec((1, LANES), lambda b, h, i: (0, 0)),
        ],
        out_specs=pl.BlockSpec((None, t, LANES), lambda b, h, i: (b, i, h)),
        out_shape=jax.ShapeDtypeStruct((bsz, seq, A_HEADS * LANES), BF16),
        compiler_params=_cparams("parallel", "parallel", "arbitrary"),
    )(lamv, qk, qk, vv, wd, wp, far, gain.reshape(1, LANES).astype(F32))


def _gmlp_kernel(u_ref, v_ref, w_ref, b_ref, o_ref, *, chunks):
    row = lax.broadcasted_iota(I32, (B_CHUNK, B_CHUNK), 0)
    col = lax.broadcasted_iota(I32, (B_CHUNK, B_CHUNK), 1)
    ws = [jnp.where(row >= col, w_ref[g], 0.0).astype(BF16) for g in range(B_GROUPS)]
    for c in range(chunks):
        rs = slice(c * B_CHUNK, (c + 1) * B_CHUNK)
        for g in range(B_GROUPS):
            cs = slice(g * B_GROUP_DIM, (g + 1) * B_GROUP_DIM)
            sv = jnp.dot(ws[g], v_ref[rs, cs], preferred_element_type=F32) + b_ref[:, cs]
            o_ref[rs, cs] = (u_ref[rs, cs] * sv).astype(o_ref.dtype)


def _gmlp(u, v, w_s, b_s):
    n, width = u.shape
    chunks = 4
    tm = chunks * B_CHUNK
    bfull = jnp.repeat(b_s.T.astype(F32), B_GROUP_DIM, axis=1)
    return pl.pallas_call(
        functools.partial(_gmlp_kernel, chunks=chunks),
        grid=(n // tm,),
        in_specs=[
            pl.BlockSpec((tm, width), lambda i: (i, 0)),
            pl.BlockSpec((tm, width), lambda i: (i, 0)),
            pl.BlockSpec((B_GROUPS, B_CHUNK, B_CHUNK), lambda i: (0, 0, 0)),
            pl.BlockSpec((B_CHUNK, width), lambda i: (0, 0)),
        ],
        out_specs=pl.BlockSpec((tm, width), lambda i: (i, 0)),
        out_shape=jax.ShapeDtypeStruct((n, width), BF16),
        compiler_params=_cparams("parallel"),
    )(u, v, w_s.astype(F32), bfull)


def _dsa_kernel(cq_ref, ck_ref, cv_ref, iq_ref, ikk_ref, iwq_ref, wd_ref, wp_ref, c_ref, o_ref,
                key_ref, *, t, seq, k_sel, scale):
    i = pl.program_id(1)
    nt = seq // t
    qpos = i * t + lax.broadcasted_iota(I32, (t, seq), 0)
    kpos = lax.broadcasted_iota(I32, (t, seq), 1)
    valid = kpos <= qpos

    ikk = ikk_ref[...].astype(BF16)
    iw = iwq_ref[...]
    score = jnp.zeros((t, seq), F32)
    for pair in range(C_IDX_HEADS // 2):
        qp = iq_ref[:, pair * LANES:(pair + 1) * LANES].astype(BF16)
        lo = lax.broadcasted_iota(I32, qp.shape, 1) < C_IDX_DIM
        zero = jnp.zeros_like(qp)
        for half in range(2):
            h = 2 * pair + half
            qh = jnp.where(lo, qp, zero) if half == 0 else jnp.where(lo, zero, qp)
            d = lax.dot_general(qh, ikk, (((1,), (1,)), ((), ())), preferred_element_type=F32)
            score = score + jnp.maximum(d, 0.0) * iw[:, h:h + 1]

    bits = pltpu.bitcast(score + 0.0, I32)
    skey = bits ^ ((bits >> 31) & 0x7FFFFFFF)
    key_ref[...] = jnp.where(valid, skey, INT_MIN)

    def count(mask):
        return jnp.sum(jnp.where(mask, 1.0, 0.0), axis=-1, keepdims=True)

    def vbody(it, p_u):
        cand = p_u | (jnp.int32(1) << (31 - it))
        cnt = count(key_ref[...] >= (cand ^ INT_MIN))
        return jnp.where(cnt >= k_sel, cand, p_u)

    p_u = lax.fori_loop(0, 32, vbody, jnp.zeros((t, 1), I32))
    thr = p_u ^ INT_MIN
    keys = key_ref[...]
    gt = keys > thr
    eq = (keys == thr) & valid
    need = k_sel - count(gt)

    nbits = max(1, (seq - 1).bit_length())

    def ibody(it, m):
        cand = m | (jnp.int32(1) << (nbits - 1 - it))
        cnt = count(eq & (kpos < cand))
        return jnp.where(cnt < need, cand, m)

    m = lax.fori_loop(0, nbits, ibody, jnp.zeros((t, 1), I32))
    sel = valid & (gt | (eq & (kpos <= m)))

    ck = ck_ref[...]
    cv = cv_ref[...]
    for h in range(C_HEADS):
        far = c_ref[h, 0:1, 0:1]
        tiles = []
        for jt in range(nt):
            tiles.append(jnp.where(i == jt, wd_ref[h], jnp.where(i == jt + 1, wp_ref[h], far)))
        bias = tiles[0] if nt == 1 else jnp.concatenate(tiles, -1)
        s = lax.dot_general(cq_ref[:, h * C_DIM:(h + 1) * C_DIM], ck, (((1,), (1,)), ((), ())),
                            preferred_element_type=F32)
        s = jnp.where(sel, s * scale + bias, NEG)
        p = jnp.exp(s - jnp.max(s, -1, keepdims=True))
        p = p / jnp.sum(p, -1, keepdims=True)
        o_ref[:, h * C_DIM:(h + 1) * C_DIM] = jnp.dot(
            p.astype(BF16), cv, preferred_element_type=F32).astype(o_ref.dtype)


def _dsa(cqk, vv, idx, bias, bsz, seq, k_sel):
    t = LANES
    wd, wp, far = bias
    nq = seq // t
    return pl.pallas_call(
        functools.partial(_dsa_kernel, t=t, seq=seq, k_sel=k_sel, scale=C_DIM ** -0.5),
        grid=(bsz, nq),
        in_specs=[
            pl.BlockSpec((None, t, C_HEADS * C_DIM), lambda b, i: (b, i, 0)),
            pl.BlockSpec((None, seq, C_DIM), lambda b, i: (b, 0, C_HEADS)),
            pl.BlockSpec((None, seq, C_DIM), lambda b, i: (b, 0, C_HEADS)),
            pl.BlockSpec((None, t, C_IDX_HEADS * C_IDX_DIM), lambda b, i: (b, i, 0)),
            pl.BlockSpec((None, seq, LANES), lambda b, i: (b, 0, 4)),
            pl.BlockSpec((None, t, LANES), lambda b, i: (b, i, 5)),
            pl.BlockSpec((C_HEADS, t, t), lambda b, i: (0, 0, 0)),
            pl.BlockSpec((C_HEADS, t, t), lambda b, i: (0, 0, 0)),
            pl.BlockSpec((C_HEADS, 1, LANES), lambda b, i: (0, 0, 0)),
        ],
        out_specs=pl.BlockSpec((None, t, C_HEADS * C_DIM), lambda b, i: (b, i, 0)),
        out_shape=jax.ShapeDtypeStruct((bsz, seq, C_HEADS * C_DIM), BF16),
        scratch_shapes=[pltpu.VMEM((t, seq), I32)],
        compiler_params=_cparams("parallel", "arbitrary"),
    )(cqk, cqk, vv, idx, idx, idx, wd, wp, far)


def _merge_kernel(x_ref, ya_ref, yb_ref, yc_ref, g_ref, wa_ref, wb_ref, wc_ref, wo_ref, o_ref, *, d):
    merged = (g_ref[:, 0:d] * jnp.dot(ya_ref[...], wa_ref[...], preferred_element_type=F32)
              + g_ref[:, d:2 * d] * jnp.dot(yb_ref[...], wb_ref[...], preferred_element_type=F32)
              + g_ref[:, 2 * d:3 * d] * jnp.dot(yc_ref[...], wc_ref[...], preferred_element_type=F32))
    o_ref[...] = x_ref[...] + jnp.dot(merged.astype(BF16), wo_ref[...], preferred_element_type=F32)


def _merge(x, ya, yb, yc, gates, wa, wb, wc, wo):
    n, d = x.shape
    bw = ya.shape[1]
    tm = min(512, n)
    row = lambda w: pl.BlockSpec((tm, w), lambda i: (i, 0))
    full = lambda a, b: pl.BlockSpec((a, b), lambda i: (0, 0))
    return pl.pallas_call(
        functools.partial(_merge_kernel, d=d),
        grid=(n // tm,),
        in_specs=[row(d), row(bw), row(bw), row(bw), row(3 * d),
                  full(bw, d), full(bw, d), full(bw, d), full(d, d)],
        out_specs=row(d),
        out_shape=jax.ShapeDtypeStruct((n, d), F32),
        compiler_params=_cparams("parallel"),
    )(x, ya, yb, yc, gates, wa.astype(BF16), wb.astype(BF16), wc.astype(BF16), wo.astype(BF16))


def _mem_attn_kernel(x_ref, q_ref, k_ref, v_ref, wo_ref, o_ref, *, scale):
    outs = []
    for h in range(M_HEADS):
        cs = slice(h * M_DIM, (h + 1) * M_DIM)
        s = lax.dot_general(q_ref[:, cs], k_ref[:, cs], (((1,), (1,)), ((), ())),
                            preferred_element_type=F32) * scale
        p = jnp.exp(s - jnp.max(s, -1, keepdims=True))
        p = p / jnp.sum(p, -1, keepdims=True)
        outs.append(jnp.dot(p.astype(BF16), v_ref[:, cs], preferred_element_type=F32).astype(BF16))
    o = jnp.concatenate(outs, -1)
    o_ref[...] = x_ref[...] + jnp.dot(o, wo_ref[...], preferred_element_type=F32)


def _mem_attn(x, q, k, v, wo, bsz, seq, mlen):
    d = x.shape[-1]
    w = M_HEADS * M_DIM
    t = min(512, seq)
    return pl.pallas_call(
        functools.partial(_mem_attn_kernel, scale=M_DIM ** -0.5),
        grid=(bsz, seq // t),
        in_specs=[
            pl.BlockSpec((None, t, d), lambda b, i: (b, i, 0)),
            pl.BlockSpec((None, t, w), lambda b, i: (b, i, 0)),
            pl.BlockSpec((None, mlen, w), lambda b, i: (b, 0, 0)),
            pl.BlockSpec((None, mlen, w), lambda b, i: (b, 0, 0)),
            pl.BlockSpec((w, d), lambda b, i: (0, 0)),
        ],
        out_specs=pl.BlockSpec((None, t, d), lambda b, i: (b, i, 0)),
        out_shape=jax.ShapeDtypeStruct((bsz, seq, d), F32),
        compiler_params=_cparams("parallel", "arbitrary"),
    )(x, q, k, v, wo.astype(BF16))


def _topk_rows(arr, k):
    nrow = arr.shape[0]
    rid = lax.broadcasted_iota(I32, arr.shape, 0)
    vals, idxs = [], []
    for _ in range(k):
        m = jnp.max(arr, axis=0, keepdims=True)
        am = jnp.min(jnp.where(arr == m, rid, nrow), axis=0, keepdims=True)
        vals.append(m)
        idxs.append(am)
        arr = jnp.where(rid == am, -jnp.inf, arr)
    return jnp.concatenate(vals, 0), jnp.concatenate(idxs, 0)


def _pick_rows(table, sel):
    out = jnp.zeros(sel.shape, table.dtype)
    for r in range(table.shape[0]):
        out = jnp.where(sel == r, table[r:r + 1, :], out)
    return out


def _peer_topk_kernel(q_ref, k1_ref, k2_ref, ids_ref, gate_ref):
    q = q_ref[...].astype(BF16)
    half = P_QDIM // 2
    dn = (((1,), (1,)), ((), ()))
    s1 = lax.dot_general(k1_ref[...], q[:, :half], dn, preferred_element_type=F32)
    s2 = lax.dot_general(k2_ref[...], q[:, half:], dn, preferred_element_type=F32)
    v1, i1 = _topk_rows(s1, P_TOPK)
    v2, i2 = _topk_rows(s2, P_TOPK)
    cand = jnp.concatenate([v1[a:a + 1, :] + v2 for a in range(P_TOPK)], 0)
    cs, ci = _topk_rows(cand, P_TOPK)
    e1 = _pick_rows(i1, ci >> 4)
    e2 = _pick_rows(i2, ci & (P_TOPK - 1))
    ids_ref[...] = e1 * P_NKEYS + e2
    p = jnp.exp(cs - jnp.max(cs, axis=0, keepdims=True))
    gate_ref[...] = p / jnp.sum(p, axis=0, keepdims=True)


def _peer_topk(q, sk1, sk2):
    n = q.shape[0]
    t = LANES
    assert P_TOPK == 16
    return pl.pallas_call(
        _peer_topk_kernel,
        grid=(n // t, P_HEADS),
        in_specs=[
            pl.BlockSpec((t, P_QDIM), lambda i, h: (i, h)),
            pl.BlockSpec((P_NKEYS, P_QDIM // 2), lambda i, h: (0, 0)),
            pl.BlockSpec((P_NKEYS, P_QDIM // 2), lambda i, h: (0, 0)),
        ],
        out_specs=[
            pl.BlockSpec((P_TOPK, t), lambda i, h: (h, i)),
            pl.BlockSpec((P_TOPK, t), lambda i, h: (h, i)),
        ],
        out_shape=[
            jax.ShapeDtypeStruct((P_HEADS * P_TOPK, n), I32),
            jax.ShapeDtypeStruct((P_HEADS * P_TOPK, n), F32),
        ],
        compiler_params=_cparams("parallel", "arbitrary"),
    )(q, sk1.astype(BF16), sk2.astype(BF16))


PEER_TOKENS_PER_STEP = 128
PEER_TOKENS_PER_SLOT = 8


def _pack_tables(u_tab, v_tab):
    ub = lax.bitcast_convert_type(u_tab.astype(BF16), jnp.uint16).astype(jnp.uint32)
    vb = lax.bitcast_convert_type(v_tab.astype(BF16), jnp.uint16).astype(jnp.uint32)
    return ub | (vb << 16)


def _peer_expert_kernel(ids_hbm, x_ref, g_ref, gate_ref, uv_hbm, o_ref,
                        ids_smem, xn_ref, buf0, buf1, sem_ids, sem_row, *, tt, tb, npair, nsteps):
    i = pl.program_id(0)
    rows = tb * npair
    nsub = tt // tb
    m = tt * npair
    bufs = (buf0, buf1)
    cur = i & 1
    has_next = i + 1 < nsteps

    def ids_copy(step, half):
        return pltpu.make_async_copy(ids_hbm.at[step], ids_smem.at[pl.ds(half * m, m)], sem_ids.at[half])

    def issue_token(ids_base, j, dst, slot):
        for r in range(npair):
            e = ids_smem[ids_base + j * npair + r]
            pltpu.make_async_copy(uv_hbm.at[pl.ds(e, 1)], dst.at[pl.ds(j * npair + r, 1)],
                                  sem_row.at[slot]).start(priority=r % 2)

    def wait_rows(slot):
        pltpu.make_async_copy(uv_hbm.at[pl.ds(0, rows)], bufs[slot], sem_row.at[slot]).wait()

    @pl.when(i == 0)
    def _():
        first = ids_copy(0, 0)
        first.start()
        first.wait()

        def body(j, carry):
            issue_token(0, j, buf0, 0)
            return carry
        lax.fori_loop(0, tb, body, 0)

    @pl.when(has_next)
    def _():
        ids_copy(i + 1, 1 - cur).start()

    x = x_ref[...]
    ms = jnp.mean(x * x, axis=-1, keepdims=True)
    xn_ref[...] = x * lax.rsqrt(ms + EPS) * g_ref[...]
    lane = lax.broadcasted_iota(I32, (npair, tt), 1)

    def sub_batch(sb, slot):
        src, dst = bufs[slot], bufs[1 - slot]
        wait_rows(slot)
        is_last = sb == nsub - 1

        @pl.when(jnp.logical_and(is_last, has_next))
        def _():
            ids_copy(i + 1, 1 - cur).wait()

        nxt_half = jnp.where(jnp.logical_and(is_last, has_next), 1 - cur, cur)
        nxt_sb = jnp.where(is_last, jnp.where(has_next, 0, sb), sb + 1)
        ids_base = nxt_half * m + nxt_sb * rows

        def body(j, carry):
            issue_token(ids_base, j, dst, 1 - slot)
            tok = sb * tb + j
            r0 = pl.multiple_of(j * npair, npair)
            xrow = xn_ref[pl.ds(tok, 1), :]
            w = src[pl.ds(r0, npair), :]
            u = pltpu.bitcast(w << 16, F32)
            v = pltpu.bitcast(w & jnp.uint32(0xFFFF0000), F32)
            hdot = jnp.sum(u * xrow, axis=-1, keepdims=True)
            gcol = jnp.sum(jnp.where(lane == tok, gate_ref[...], 0.0), axis=-1, keepdims=True)
            coef = gcol * _gelu(hdot)
            out = jnp.sum(v * coef, axis=0, keepdims=True)
            o_ref[pl.ds(tok, 1), :] = x_ref[pl.ds(tok, 1), :] + out
            return carry
        lax.fori_loop(0, tb, body, 0)

    def pair(sp, carry):
        sub_batch(2 * sp, 0)
        sub_batch(2 * sp + 1, 1)
        return carry

    lax.fori_loop(0, nsub // 2, pair, 0)

    @pl.when(i == nsteps - 1)
    def _():
        wait_rows(0)


def _peer_expert(x, g, ids_t, gates_t, uv_tab):
    n, d = x.shape
    npair = ids_t.shape[0]
    tt, tb = PEER_TOKENS_PER_STEP, PEER_TOKENS_PER_SLOT
    assert n % tt == 0 and tt % (2 * tb) == 0
    nsteps = n // tt
    ids = ids_t.T.reshape(nsteps, tt * npair)
    return pl.pallas_call(
        functools.partial(_peer_expert_kernel, tt=tt, tb=tb, npair=npair, nsteps=nsteps),
        grid=(nsteps,),
        in_specs=[
            pl.BlockSpec(memory_space=pl.ANY),
            pl.BlockSpec((tt, d), lambda i: (i, 0)),
            pl.BlockSpec((1, d), lambda i: (0, 0)),
            pl.BlockSpec((npair, tt), lambda i: (0, i)),
            pl.BlockSpec(memory_space=pl.ANY),
        ],
        out_specs=pl.BlockSpec((tt, d), lambda i: (i, 0)),
        out_shape=jax.ShapeDtypeStruct((n, d), F32),
        scratch_shapes=[
            pltpu.SMEM((2 * tt * npair,), I32),
            pltpu.VMEM((tt, d), F32),
            pltpu.VMEM((tb * npair, d), jnp.uint32),
            pltpu.VMEM((tb * npair, d), jnp.uint32),
            pltpu.SemaphoreType.DMA((2,)),
            pltpu.SemaphoreType.DMA((2,)),
        ],
        compiler_params=_cparams("arbitrary"),
    )(ids, x, g.reshape(1, d).astype(F32), gates_t, uv_tab)


def _tile_gain(g, reps):
    return jnp.tile(g.astype(F32), reps)


def _layer(l, x, memn_kv, bias_a, bias_c, p):
    bsz, seq, d = x.shape
    n = bsz * seq
    lam_init = 0.8 - 0.6 * math.exp(-0.3 * l)
    x2 = x.reshape(n, d)
    w_in = p["w_in"][l]
    cols = np.cumsum([0, 512, 512, 512, 512, 512, 512, 128, 128, 512, 64, 8, 3 * d])
    seg = lambda a, b: w_in[:, cols[a]:cols[b]]
    gmix = p["norm_mix"][l]

    qk = _proj(x2, gmix, seg(0, 2), "norm64",
               jnp.concatenate([_tile_gain(p["a_q_gain"][l], 8), _tile_gain(p["a_k_gain"][l], 8)]), BF16)
    vv = _proj(x2, gmix, jnp.concatenate([seg(2, 3), seg(7, 8)], 1), "none", None, BF16, tn=640)
    u = _proj(x2, gmix, seg(3, 4), "gelu", None, F32)
    v = _proj(x2, gmix, seg(4, 5), "gelu_norm128", p["b_v_gain"][l], BF16)
    cqk = _proj(x2, gmix, seg(5, 7), "norm128",
                jnp.concatenate([_tile_gain(p["c_q_gain"][l], 4), p["c_k_gain"][l].astype(F32)]), BF16, tn=640)
    w_idx = jnp.concatenate([seg(8, 9), seg(9, 10), seg(9, 10), seg(10, 11),
                             jnp.zeros((d, 120), w_in.dtype)], 1)
    idx = _proj(x2, gmix, w_idx, "none", None, F32, tn=768)
    gates = _proj(x2, gmix, seg(11, 12), "sigmoid", None, F32)

    lamv = jnp.zeros((8, LANES), F32)
    for r, name in enumerate(("a_lq1", "a_lk1", "a_lq2", "a_lk2")):
        lamv = lamv.at[r, :A_DIM].set(p[name][l].astype(F32))
    ya = _diff_attn(qk.reshape(bsz, seq, -1), vv.reshape(bsz, seq, -1), lamv, bias_a,
                    p["a_subln_gain"][l], lam_init, bsz, seq)
    yb = _gmlp(u, v, p["b_w_s"][l], p["b_b_s"][l])
    k_sel = min(C_TOPK_MAX, seq // 4)
    yc = _dsa(cqk.reshape(bsz, seq, -1), vv.reshape(bsz, seq, -1), idx.reshape(bsz, seq, -1),
              bias_c, bsz, seq, k_sel)
    x2 = _merge(x2, ya.reshape(n, -1), yb, yc.reshape(n, -1), gates,
                p["w_br_a"][l], p["w_br_b"][l], p["w_br_c"][l], p["w_mix_out"][l])

    mk, mv = memn_kv
    mq = _proj(x2, p["norm_mem"][l], p["m_wq"][l], "norm128", _tile_gain(p["m_q_gain"][l], M_HEADS), BF16)
    x3 = _mem_attn(x2.reshape(bsz, seq, d), mq.reshape(bsz, seq, -1), mk, mv, p["m_wo"][l],
                   bsz, seq, mk.shape[1])
    x2 = x3.reshape(n, d)

    pq = _proj(x2, p["norm_peer"][l], p["p_wq"][l], "none", None, F32)
    ids_t, gates_t = _peer_topk(pq, p["p_subkey1"][l], p["p_subkey2"][l])
    x2 = _peer_expert(x2, p["norm_peer"][l], ids_t, gates_t, _pack_tables(p["p_u"][l], p["p_v"][l]))
    return x2.reshape(bsz, seq, d)


def kernel(x, mem, rel_bias, norm_mix, w_in, a_q_gain, a_k_gain, a_lq1, a_lk1, a_lq2, a_lk2, a_subln_gain, b_v_gain, b_w_s, b_b_s, c_q_gain, c_k_gain, w_br_a, w_br_b, w_br_c, w_mix_out, norm_mem, norm_memsrc, m_wq, m_wkv, m_q_gain, m_k_gain, m_wo, norm_peer, p_wq, p_subkey1, p_subkey2, p_u, p_v):
    p = dict(norm_mix=norm_mix, w_in=w_in, a_q_gain=a_q_gain, a_k_gain=a_k_gain, a_lq1=a_lq1,
             a_lk1=a_lk1, a_lq2=a_lq2, a_lk2=a_lk2, a_subln_gain=a_subln_gain, b_v_gain=b_v_gain,
             b_w_s=b_w_s, b_b_s=b_b_s, c_q_gain=c_q_gain, c_k_gain=c_k_gain, w_br_a=w_br_a,
             w_br_b=w_br_b, w_br_c=w_br_c, w_mix_out=w_mix_out, norm_mem=norm_mem, m_wq=m_wq,
             m_q_gain=m_q_gain, m_wo=m_wo, norm_peer=norm_peer, p_wq=p_wq, p_subkey1=p_subkey1,
             p_subkey2=p_subkey2, p_u=p_u, p_v=p_v)
    bsz, seq, d = x.shape
    mlen = mem.shape[1]
    depth = w_in.shape[0]
    bias_a = _bias_tiles(rel_bias, slice(0, A_HEADS), min(256, seq), seq)
    bias_c = _bias_tiles(rel_bias, slice(A_HEADS, A_HEADS + C_HEADS), LANES, seq)
    mem2 = mem.reshape(bsz * mlen, d)
    w = M_HEADS * M_DIM
    for l in range(depth):
        mk = _proj(mem2, norm_memsrc[l], m_wkv[l][:, :w], "norm128", _tile_gain(m_k_gain[l], M_HEADS), BF16)
        mv = _proj(mem2, norm_memsrc[l], m_wkv[l][:, w:], "none", None, BF16)
        x = _layer(l, x, (mk.reshape(bsz, mlen, w), mv.reshape(bsz, mlen, w)), bias_a, bias_c, p)
    return x
```

```python
import functools
import math

import jax
import jax.numpy as jnp
import numpy as np
from jax import lax
from jax.experimental import pallas as pl
from jax.experimental.pallas import tpu as pltpu

F32 = jnp.float32
BF16 = jnp.bfloat16
I32 = jnp.int32

EPS = 1e-6
NEG = -1e30
INT_MIN = -(2 ** 31)

A_HEADS = 4
A_DIM = 64
B_GROUPS = 4
B_GROUP_DIM = 128
B_CHUNK = 128
C_HEADS = 4
C_DIM = 128
C_IDX_HEADS = 8
C_IDX_DIM = 64
C_TOPK_MAX = 256
M_HEADS = 4
M_DIM = 128
P_HEADS = 8
P_QDIM = 256
P_NKEYS = 128
P_TOPK = 16
REL_BUCKETS = 32
REL_MAX_DIST = 128
LANES = 128

DSA_KEY_RANGES = 8

VMEM_LIMIT = 56 * 1024 * 1024


def _cparams(*sem):
    return pltpu.CompilerParams(dimension_semantics=sem, vmem_limit_bytes=VMEM_LIMIT)


def _gelu(x):
    return 0.5 * x * (1.0 + lax.erf(x * (1.0 / math.sqrt(2.0))))


def _group_norm(h, gain, group):
    parts = []
    for c in range(h.shape[-1] // LANES):
        hc = h[:, c * LANES:(c + 1) * LANES]
        sq = hc * hc
        if group == LANES:
            r = lax.rsqrt(jnp.sum(sq, -1, keepdims=True) * (1.0 / LANES) + EPS)
        else:
            lo = lax.broadcasted_iota(I32, hc.shape, 1) < 64
            s_lo = jnp.sum(jnp.where(lo, sq, 0.0), -1, keepdims=True)
            s_hi = jnp.sum(jnp.where(lo, 0.0, sq), -1, keepdims=True)
            r = jnp.where(lo, lax.rsqrt(s_lo * (1.0 / 64) + EPS), lax.rsqrt(s_hi * (1.0 / 64) + EPS))
        parts.append(hc * r)
    out = parts[0] if len(parts) == 1 else jnp.concatenate(parts, -1)
    return out * gain


def _epilogue(h, gain, mode):
    if mode == "none":
        return h
    if mode == "norm64":
        return _group_norm(h, gain, 64)
    if mode == "norm128":
        return _group_norm(h, gain, 128)
    if mode == "gelu":
        return _gelu(h)
    if mode == "gelu_norm128":
        return _group_norm(_gelu(h), gain, 128)
    if mode == "sigmoid":
        return jax.nn.sigmoid(h)
    raise ValueError(mode)


def _proj_kernel(x_ref, g_ref, w_ref, e_ref, o_ref, xn_ref, *, mode):
    @pl.when(pl.program_id(1) == 0)
    def _():
        x = x_ref[...]
        ms = jnp.mean(x * x, axis=-1, keepdims=True)
        xn_ref[...] = (x * lax.rsqrt(ms + EPS) * g_ref[...]).astype(BF16)

    h = jnp.dot(xn_ref[...], w_ref[...], preferred_element_type=F32)
    o_ref[...] = _epilogue(h, e_ref[...], mode).astype(o_ref.dtype)


def _proj(x, g, w, mode="none", gain=None, out_dtype=F32, tn=512):
    n, d = x.shape
    dout = w.shape[1]
    tm = min(1024, n)
    assert n % tm == 0 and dout % tn == 0, (n, dout, tn)
    if gain is None:
        gain = jnp.ones((dout,), F32)
    return pl.pallas_call(
        functools.partial(_proj_kernel, mode=mode),
        grid=(n // tm, dout // tn),
        in_specs=[
            pl.BlockSpec((tm, d), lambda i, j: (i, 0)),
            pl.BlockSpec((1, d), lambda i, j: (0, 0)),
            pl.BlockSpec((d, tn), lambda i, j: (0, j)),
            pl.BlockSpec((1, tn), lambda i, j: (0, j)),
        ],
        out_specs=pl.BlockSpec((tm, tn), lambda i, j: (i, j)),
        out_shape=jax.ShapeDtypeStruct((n, dout), out_dtype),
        scratch_shapes=[pltpu.VMEM((tm, d), BF16)],
        compiler_params=_cparams("parallel", "arbitrary"),
    )(x, g.reshape(1, d).astype(F32), w.astype(BF16), gain.reshape(1, dout).astype(F32))


def _t5_bucket(dist):
    n = jnp.maximum(dist, 0)
    max_exact = REL_BUCKETS // 2
    nf = jnp.maximum(n, 1).astype(F32)
    large = max_exact + (jnp.log(nf / max_exact) / math.log(REL_MAX_DIST / max_exact)
                         * (REL_BUCKETS - max_exact)).astype(I32)
    large = jnp.minimum(large, REL_BUCKETS - 1)
    return jnp.where(n < max_exact, n, large)


def _far_bucket_is_constant(t, seq):
    d = np.arange(t + 1, max(seq, t + 2), dtype=np.float64)
    max_exact = REL_BUCKETS // 2
    large = max_exact + np.floor(np.log(d / max_exact) / math.log(REL_MAX_DIST / max_exact)
                                 * (REL_BUCKETS - max_exact) + 1e-6).astype(np.int64)
    safe = max_exact + np.floor(np.log(d / max_exact) / math.log(REL_MAX_DIST / max_exact)
                                * (REL_BUCKETS - max_exact) - 1e-3).astype(np.int64)
    return bool(np.all(np.minimum(large, REL_BUCKETS - 1) == REL_BUCKETS - 1)
                and np.all(np.minimum(safe, REL_BUCKETS - 1) == REL_BUCKETS - 1))


def _bias_tiles(rel_bias, heads, t, seq):
    assert _far_bucket_is_constant(t, seq)
    tab = rel_bias[_t5_bucket(jnp.arange(2 * t, dtype=I32))][:, heads].astype(F32)
    i = jnp.arange(t, dtype=I32)[:, None]
    j = jnp.arange(t, dtype=I32)[None, :]
    d0 = i - j
    diag = jnp.where((d0 >= 0)[..., None], tab[jnp.maximum(d0, 0)], NEG)
    prev = tab[t + d0]
    far = rel_bias[REL_BUCKETS - 1, heads].astype(F32)
    nh = far.shape[0]
    return (jnp.transpose(diag, (2, 0, 1)), jnp.transpose(prev, (2, 0, 1)),
            jnp.broadcast_to(far[:, None, None], (nh, 1, LANES)))


def _diff_attn_kernel(lam_ref, q_ref, k_ref, v_ref, wd_ref, wp_ref, c_ref, g_ref, o_ref, *,
                      t, scale, lam_init):
    i = pl.program_id(2)
    q = q_ref[...]
    lo = lax.broadcasted_iota(I32, q.shape, 1) < A_DIM
    zero = jnp.zeros_like(q)
    qs = (jnp.where(lo, q, zero), jnp.where(lo, zero, q))
    far = c_ref[0:1, 0:1]

    def tile(start, bias, carry):
        k = k_ref[pl.ds(start, t), :]
        v = v_ref[pl.ds(start, t), :]
        out = []
        for m in range(2):
            mo, lo_, ao = carry[m]
            s = lax.dot_general(qs[m], k, (((1,), (1,)), ((), ())), preferred_element_type=F32)
            s = s * scale + bias
            mn = jnp.maximum(mo, jnp.max(s, -1, keepdims=True))
            a = jnp.exp(mo - mn)
            p = jnp.exp(s - mn)
            ln = a * lo_ + jnp.sum(p, -1, keepdims=True)
            an = a * ao + jnp.dot(p.astype(BF16), v, preferred_element_type=F32)
            out.append((mn, ln, an))
        return tuple(out)

    one = (jnp.full((t, 1), NEG, F32), jnp.zeros((t, 1), F32), jnp.zeros((t, LANES), F32))
    carry = lax.fori_loop(0, jnp.maximum(i - 1, 0),
                          lambda j, cr: tile(pl.multiple_of(j * t, t), far, cr), (one, one))
    jp = jnp.maximum(i - 1, 0)
    bias_p = jnp.where(i > 0, wp_ref[...], NEG)
    carry = tile(pl.multiple_of(jp * t, t), bias_p, carry)
    carry = tile(pl.multiple_of(i * t, t), wd_ref[...], carry)

    lv = lam_ref[...]
    lam = (jnp.exp(jnp.sum(lv[0:1] * lv[1:2], -1, keepdims=True))
           - jnp.exp(jnp.sum(lv[2:3] * lv[3:4], -1, keepdims=True)) + lam_init)
    o = carry[0][2] / carry[0][1] - lam * (carry[1][2] / carry[1][1])
    r = lax.rsqrt(jnp.mean(o * o, -1, keepdims=True) + EPS)
    o_ref[...] = ((o * r * g_ref[...]) * (1.0 - lam_init)).astype(o_ref.dtype)


def _diff_attn(qk, vv, lamv, bias, gain, lam_init, bsz, seq):
    t = min(256, seq)
    nq = seq // t
    wd, wp, far = bias
    return pl.pallas_call(
        functools.partial(_diff_attn_kernel, t=t, scale=A_DIM ** -0.5, lam_init=lam_init),
        grid=(bsz, A_HEADS, nq),
        in_specs=[
            pl.BlockSpec((8, LANES), lambda b, h, i: (0, 0)),
            pl.BlockSpec((None, t, LANES), lambda b, h, i: (b, i, h)),
            pl.BlockSpec((None, seq, LANES), lambda b, h, i: (b, 0, A_HEADS + h)),
            pl.BlockSpec((None, seq, LANES), lambda b, h, i: (b, 0, h)),
            pl.BlockSpec((None, t, t), lambda b, h, i: (h, 0, 0)),
            pl.BlockSpec((None, t, t), lambda b, h, i: (h, 0, 0)),
            pl.BlockSpec((None, 1, LANES), lambda b, h, i: (h, 0, 0)),
            pl.BlockSpec((1, LANES), lambda b, h, i: (0, 0)),
        ],
        out_specs=pl.BlockSpec((None, t, LANES), lambda b, h, i: (b, i, h)),
        out_shape=jax.ShapeDtypeStruct((bsz, seq, A_HEADS * LANES), BF16),
        compiler_params=_cparams("parallel", "parallel", "arbitrary"),
    )(lamv, qk, qk, vv, wd, wp, far, gain.reshape(1, LANES).astype(F32))


def _gmlp_kernel(u_ref, v_ref, w_ref, b_ref, o_ref, *, chunks):
    row = lax.broadcasted_iota(I32, (B_CHUNK, B_CHUNK), 0)
    col = lax.broadcasted_iota(I32, (B_CHUNK, B_CHUNK), 1)
    ws = [jnp.where(row >= col, w_ref[g], 0.0).astype(BF16) for g in range(B_GROUPS)]
    for c in range(chunks):
        rs = slice(c * B_CHUNK, (c + 1) * B_CHUNK)
        for g in range(B_GROUPS):
            cs = slice(g * B_GROUP_DIM, (g + 1) * B_GROUP_DIM)
            sv = jnp.dot(ws[g], v_ref[rs, cs], preferred_element_type=F32) + b_ref[:, cs]
            o_ref[rs, cs] = (u_ref[rs, cs] * sv).astype(o_ref.dtype)


def _gmlp(u, v, w_s, b_s):
    n, width = u.shape
    chunks = 4
    tm = chunks * B_CHUNK
    bfull = jnp.repeat(b_s.T.astype(F32), B_GROUP_DIM, axis=1)
    return pl.pallas_call(
        functools.partial(_gmlp_kernel, chunks=chunks),
        grid=(n // tm,),
        in_specs=[
            pl.BlockSpec((tm, width), lambda i: (i, 0)),
            pl.BlockSpec((tm, width), lambda i: (i, 0)),
            pl.BlockSpec((B_GROUPS, B_CHUNK, B_CHUNK), lambda i: (0, 0, 0)),
            pl.BlockSpec((B_CHUNK, width), lambda i: (0, 0)),
        ],
        out_specs=pl.BlockSpec((tm, width), lambda i: (i, 0)),
        out_shape=jax.ShapeDtypeStruct((n, width), BF16),
        compiler_params=_cparams("parallel"),
    )(u, v, w_s.astype(F32), bfull)


def _dsa_kernel(cq_ref, ck_ref, cv_ref, iq_ref, ikk_ref, iwq_ref, wd_ref, wp_ref, c_ref, o_ref,
                key_ref, sel_ref, *, t, seq, k_sel, scale, q0):
    i = pl.program_id(1) + q0
    nt = seq // t
    qpos = i * t + lax.broadcasted_iota(I32, (t, seq), 0)
    kpos = lax.broadcasted_iota(I32, (t, seq), 1)
    valid = kpos <= qpos

    ikk = ikk_ref[...].astype(BF16)
    iw = iwq_ref[...]
    score = jnp.zeros((t, seq), F32)
    for pair in range(C_IDX_HEADS // 2):
        qp = iq_ref[:, pair * LANES:(pair + 1) * LANES].astype(BF16)
        lo = lax.broadcasted_iota(I32, qp.shape, 1) < C_IDX_DIM
        zero = jnp.zeros_like(qp)
        for half in range(2):
            h = 2 * pair + half
            qh = jnp.where(lo, qp, zero) if half == 0 else jnp.where(lo, zero, qp)
            d = lax.dot_general(qh, ikk, (((1,), (1,)), ((), ())), preferred_element_type=F32)
            score = score + jnp.maximum(d, 0.0) * iw[:, h:h + 1]

    bits = pltpu.bitcast(score + 0.0, I32)
    skey = bits ^ ((bits >> 31) & 0x7FFFFFFF)
    key_ref[...] = jnp.where(valid, skey, INT_MIN)

    def count(mask):
        return jnp.sum(jnp.where(mask, 1.0, 0.0), axis=-1, keepdims=True)

    def vbody(it, p_u):
        cand = p_u | (jnp.int32(1) << (31 - it))
        cnt = count(key_ref[...] >= (cand ^ INT_MIN))
        return jnp.where(cnt >= k_sel, cand, p_u)

    p_u = lax.fori_loop(0, 32, vbody, jnp.zeros((t, 1), I32))
    thr = p_u ^ INT_MIN
    keys = key_ref[...]
    gt = keys > thr
    eq = (keys == thr) & valid
    need = k_sel - count(gt)
    sel_ref[...] = jnp.where(valid & (keys >= thr), 1.0, 0.0)

    nbits = max(1, (seq - 1).bit_length())

    @pl.when(jnp.max(count(eq) - need) > 0.0)
    def _():
        def ibody(it, m):
            cand = m | (jnp.int32(1) << (nbits - 1 - it))
            cnt = count(eq & (kpos < cand))
            return jnp.where(cnt < need, cand, m)

        m = lax.fori_loop(0, nbits, ibody, jnp.zeros((t, 1), I32))
        sel_ref[...] = jnp.where(valid & (gt | (eq & (kpos <= m))), 1.0, 0.0)

    sel = sel_ref[...] > 0.5

    ck = ck_ref[...]
    cv = cv_ref[...]
    for h in range(C_HEADS):
        far = c_ref[h, 0:1, 0:1]
        tiles = []
        for jt in range(nt):
            tiles.append(jnp.where(i == jt, wd_ref[h], jnp.where(i == jt + 1, wp_ref[h], far)))
        bias = tiles[0] if nt == 1 else jnp.concatenate(tiles, -1)
        s = lax.dot_general(cq_ref[:, h * C_DIM:(h + 1) * C_DIM], ck, (((1,), (1,)), ((), ())),
                            preferred_element_type=F32)
        s = jnp.where(sel, s * scale + bias, NEG)
        p = jnp.exp(s - jnp.max(s, -1, keepdims=True))
        p = p / jnp.sum(p, -1, keepdims=True)
        o_ref[:, h * C_DIM:(h + 1) * C_DIM] = jnp.dot(
            p.astype(BF16), cv, preferred_element_type=F32).astype(o_ref.dtype)


def _dsa(cqk, vv, idx, bias, bsz, seq, k_sel):
    t = LANES
    wd, wp, far = bias
    nq = seq // t
    nrange = math.gcd(nq, DSA_KEY_RANGES)
    per = nq // nrange
    outs = []
    for r in range(nrange):
        q0 = r * per
        sk = (r + 1) * per * t
        qmap = lambda b, i, q0=q0: (b, i + q0, 0)
        outs.append(pl.pallas_call(
            functools.partial(_dsa_kernel, t=t, seq=sk, k_sel=k_sel, scale=C_DIM ** -0.5, q0=q0),
            grid=(bsz, per),
            in_specs=[
                pl.BlockSpec((None, t, C_HEADS * C_DIM), qmap),
                pl.BlockSpec((None, sk, C_DIM), lambda b, i: (b, 0, C_HEADS)),
                pl.BlockSpec((None, sk, C_DIM), lambda b, i: (b, 0, C_HEADS)),
                pl.BlockSpec((None, t, C_IDX_HEADS * C_IDX_DIM), qmap),
                pl.BlockSpec((None, sk, LANES), lambda b, i: (b, 0, 4)),
                pl.BlockSpec((None, t, LANES), lambda b, i, q0=q0: (b, i + q0, 5)),
                pl.BlockSpec((C_HEADS, t, t), lambda b, i: (0, 0, 0)),
                pl.BlockSpec((C_HEADS, t, t), lambda b, i: (0, 0, 0)),
                pl.BlockSpec((C_HEADS, 1, LANES), lambda b, i: (0, 0, 0)),
            ],
            out_specs=pl.BlockSpec((None, t, C_HEADS * C_DIM), lambda b, i: (b, i, 0)),
            out_shape=jax.ShapeDtypeStruct((bsz, per * t, C_HEADS * C_DIM), BF16),
            scratch_shapes=[pltpu.VMEM((t, sk), I32), pltpu.VMEM((t, sk), F32)],
            compiler_params=_cparams("parallel", "arbitrary"),
        )(cqk, cqk, vv, idx, idx, idx, wd, wp, far))
    return outs[0] if nrange == 1 else jnp.concatenate(outs, axis=1)


def _merge_kernel(x_ref, ya_ref, yb_ref, yc_ref, g_ref, wa_ref, wb_ref, wc_ref, wo_ref, o_ref, *, d):
    merged = (g_ref[:, 0:d] * jnp.dot(ya_ref[...], wa_ref[...], preferred_element_type=F32)
              + g_ref[:, d:2 * d] * jnp.dot(yb_ref[...], wb_ref[...], preferred_element_type=F32)
              + g_ref[:, 2 * d:3 * d] * jnp.dot(yc_ref[...], wc_ref[...], preferred_element_type=F32))
    o_ref[...] = x_ref[...] + jnp.dot(merged.astype(BF16), wo_ref[...], preferred_element_type=F32)


def _merge(x, ya, yb, yc, gates, wa, wb, wc, wo):
    n, d = x.shape
    bw = ya.shape[1]
    tm = min(512, n)
    row = lambda w: pl.BlockSpec((tm, w), lambda i: (i, 0))
    full = lambda a, b: pl.BlockSpec((a, b), lambda i: (0, 0))
    return pl.pallas_call(
        functools.partial(_merge_kernel, d=d),
        grid=(n // tm,),
        in_specs=[row(d), row(bw), row(bw), row(bw), row(3 * d),
                  full(bw, d), full(bw, d), full(bw, d), full(d, d)],
        out_specs=row(d),
        out_shape=jax.ShapeDtypeStruct((n, d), F32),
        compiler_params=_cparams("parallel"),
    )(x, ya, yb, yc, gates, wa.astype(BF16), wb.astype(BF16), wc.astype(BF16), wo.astype(BF16))


def _mem_attn_kernel(x_ref, q_ref, k_ref, v_ref, wo_ref, o_ref, *, scale):
    outs = []
    for h in range(M_HEADS):
        cs = slice(h * M_DIM, (h + 1) * M_DIM)
        s = lax.dot_general(q_ref[:, cs], k_ref[:, cs], (((1,), (1,)), ((), ())),
                            preferred_element_type=F32) * scale
        p = jnp.exp(s - jnp.max(s, -1, keepdims=True))
        p = p / jnp.sum(p, -1, keepdims=True)
        outs.append(jnp.dot(p.astype(BF16), v_ref[:, cs], preferred_element_type=F32).astype(BF16))
    o = jnp.concatenate(outs, -1)
    o_ref[...] = x_ref[...] + jnp.dot(o, wo_ref[...], preferred_element_type=F32)


def _mem_attn(x, q, k, v, wo, bsz, seq, mlen):
    d = x.shape[-1]
    w = M_HEADS * M_DIM
    t = min(512, seq)
    return pl.pallas_call(
        functools.partial(_mem_attn_kernel, scale=M_DIM ** -0.5),
        grid=(bsz, seq // t),
        in_specs=[
            pl.BlockSpec((None, t, d), lambda b, i: (b, i, 0)),
            pl.BlockSpec((None, t, w), lambda b, i: (b, i, 0)),
            pl.BlockSpec((None, mlen, w), lambda b, i: (b, 0, 0)),
            pl.BlockSpec((None, mlen, w), lambda b, i: (b, 0, 0)),
            pl.BlockSpec((w, d), lambda b, i: (0, 0)),
        ],
        out_specs=pl.BlockSpec((None, t, d), lambda b, i: (b, i, 0)),
        out_shape=jax.ShapeDtypeStruct((bsz, seq, d), F32),
        compiler_params=_cparams("parallel", "arbitrary"),
    )(x, q, k, v, wo.astype(BF16))


def _topk_rows_ids(arr, ids, k):
    vals, idxs = [], []
    for _ in range(k):
        m = jnp.max(arr, axis=0, keepdims=True)
        am = jnp.min(jnp.where(arr == m, ids, 2 ** 30), axis=0, keepdims=True)
        vals.append(m)
        idxs.append(am)
        arr = jnp.where(ids == am, -jnp.inf, arr)
    return jnp.concatenate(vals, 0), jnp.concatenate(idxs, 0)


def _topk_rows(arr, k):
    return _topk_rows_ids(arr, lax.broadcasted_iota(I32, arr.shape, 0), k)


def _topk_pair_sums(v1, v2):
    k = P_TOPK
    t = v1.shape[1]
    r8 = lax.broadcasted_iota(I32, (8, t), 0)
    r16 = lax.broadcasted_iota(I32, (16, t), 0)
    vals = [v1[0:1] + v2]
    ids = [r16]
    for a in (1, 2, 3):
        vals.append(v1[a:a + 1] + v2[0:8])
        ids.append(a * k + r8)
    for b in (0, 1, 2):
        vals.append(jnp.where(r8 >= 4, v1[0:8] + v2[b:b + 1], -jnp.inf))
        ids.append(r8 * k + b)
    vals.append(v1[8:16] + v2[0:1])
    ids.append((r8 + 8) * k)
    return _topk_rows_ids(jnp.concatenate(vals, 0), jnp.concatenate(ids, 0), k)


def _pick_rows(table, sel):
    out = jnp.zeros(sel.shape, table.dtype)
    for r in range(table.shape[0]):
        out = jnp.where(sel == r, table[r:r + 1, :], out)
    return out


def _peer_topk_kernel(q_ref, k1_ref, k2_ref, ids_ref, gate_ref):
    q = q_ref[...].astype(BF16)
    half = P_QDIM // 2
    dn = (((1,), (1,)), ((), ()))
    s1 = lax.dot_general(k1_ref[...], q[:, :half], dn, preferred_element_type=F32)
    s2 = lax.dot_general(k2_ref[...], q[:, half:], dn, preferred_element_type=F32)
    v1, i1 = _topk_rows(s1, P_TOPK)
    v2, i2 = _topk_rows(s2, P_TOPK)
    cs, ci = _topk_pair_sums(v1, v2)
    e1 = _pick_rows(i1, ci >> 4)
    e2 = _pick_rows(i2, ci & (P_TOPK - 1))
    ids_ref[...] = e1 * P_NKEYS + e2
    p = jnp.exp(cs - jnp.max(cs, axis=0, keepdims=True))
    gate_ref[...] = p / jnp.sum(p, axis=0, keepdims=True)


def _peer_topk(q, sk1, sk2):
    n = q.shape[0]
    t = LANES
    assert P_TOPK == 16
    return pl.pallas_call(
        _peer_topk_kernel,
        grid=(n // t, P_HEADS),
        in_specs=[
            pl.BlockSpec((t, P_QDIM), lambda i, h: (i, h)),
            pl.BlockSpec((P_NKEYS, P_QDIM // 2), lambda i, h: (0, 0)),
            pl.BlockSpec((P_NKEYS, P_QDIM // 2), lambda i, h: (0, 0)),
        ],
        out_specs=[
            pl.BlockSpec((P_TOPK, t), lambda i, h: (h, i)),
            pl.BlockSpec((P_TOPK, t), lambda i, h: (h, i)),
        ],
        out_shape=[
            jax.ShapeDtypeStruct((P_HEADS * P_TOPK, n), I32),
            jax.ShapeDtypeStruct((P_HEADS * P_TOPK, n), F32),
        ],
        compiler_params=_cparams("parallel", "arbitrary"),
    )(q, sk1.astype(BF16), sk2.astype(BF16))


PEER_TOKENS_PER_STEP = 128
PEER_TOKENS_PER_SLOT = 8


def _pack_tables(u_tab, v_tab):
    ub = lax.bitcast_convert_type(u_tab.astype(BF16), jnp.uint16).astype(jnp.uint32)
    vb = lax.bitcast_convert_type(v_tab.astype(BF16), jnp.uint16).astype(jnp.uint32)
    return (ub | (vb << 16))[:, None, :]


def _peer_expert_kernel(ids_hbm, x_ref, g_ref, gate_ref, uv_hbm, o_ref,
                        ids_smem, xn_ref, buf0, buf1, sem_ids, sem_row, *, tt, tb, npair, nsteps):
    i = pl.program_id(0)
    rows = tb * npair
    nsub = tt // tb
    m = tt * npair
    bufs = (buf0, buf1)
    cur = i & 1
    has_next = i + 1 < nsteps

    def ids_copy(step, half):
        return pltpu.make_async_copy(ids_hbm.at[step], ids_smem.at[pl.ds(half * m, m)], sem_ids.at[half])

    def issue_token(ids_base, j, dst, slot):
        tok_base = ids_base + j * npair
        for r in range(npair):
            e = ids_smem[tok_base + r]
            pltpu.make_async_copy(uv_hbm.at[e], dst.at[j, pl.ds(r, 1)],
                                  sem_row.at[slot]).start(priority=r % 2)

    def wait_rows(slot):
        pltpu.make_async_copy(bufs[1 - slot], bufs[slot], sem_row.at[slot]).wait()

    @pl.when(i == 0)
    def _():
        first = ids_copy(0, 0)
        first.start()
        first.wait()

        def body(j, carry):
            issue_token(0, j, buf0, 0)
            return carry
        lax.fori_loop(0, tb, body, 0)

    @pl.when(has_next)
    def _():
        ids_copy(i + 1, 1 - cur).start()

    x = x_ref[...]
    ms = jnp.mean(x * x, axis=-1, keepdims=True)
    xn_ref[...] = x * lax.rsqrt(ms + EPS) * g_ref[...]
    lane = lax.broadcasted_iota(I32, (npair, tt), 1)

    def sub_batch(sb, slot):
        src, dst = bufs[slot], bufs[1 - slot]
        wait_rows(slot)
        is_last = sb == nsub - 1

        @pl.when(jnp.logical_and(is_last, has_next))
        def _():
            ids_copy(i + 1, 1 - cur).wait()

        nxt_half = jnp.where(jnp.logical_and(is_last, has_next), 1 - cur, cur)
        nxt_sb = jnp.where(is_last, jnp.where(has_next, 0, sb), sb + 1)
        ids_base = nxt_half * m + nxt_sb * rows

        def body(j, carry):
            issue_token(ids_base, j, dst, 1 - slot)
            tok = sb * tb + j
            xrow = xn_ref[pl.ds(tok, 1), :]
            w = src[j]
            u = pltpu.bitcast(w << 16, F32)
            v = pltpu.bitcast(w & jnp.uint32(0xFFFF0000), F32)
            hdot = jnp.sum(u * xrow, axis=-1, keepdims=True)
            gcol = jnp.sum(jnp.where(lane == tok, gate_ref[...], 0.0), axis=-1, keepdims=True)
            coef = gcol * _gelu(hdot)
            out = jnp.sum(v * coef, axis=0, keepdims=True)
            o_ref[pl.ds(tok, 1), :] = x_ref[pl.ds(tok, 1), :] + out
            return carry
        lax.fori_loop(0, tb, body, 0)

    def pair(sp, carry):
        sub_batch(2 * sp, 0)
        sub_batch(2 * sp + 1, 1)
        return carry

    lax.fori_loop(0, nsub // 2, pair, 0)

    @pl.when(i == nsteps - 1)
    def _():
        wait_rows(0)


def _peer_expert(x, g, ids_t, gates_t, uv_tab):
    n, d = x.shape
    npair = ids_t.shape[0]
    tt, tb = PEER_TOKENS_PER_STEP, PEER_TOKENS_PER_SLOT
    assert n % tt == 0 and tt % (2 * tb) == 0
    nsteps = n // tt
    ids = ids_t.T.reshape(nsteps, tt * npair)
    return pl.pallas_call(
        functools.partial(_peer_expert_kernel, tt=tt, tb=tb, npair=npair, nsteps=nsteps),
        grid=(nsteps,),
        in_specs=[
            pl.BlockSpec(memory_space=pl.ANY),
            pl.BlockSpec((tt, d), lambda i: (i, 0)),
            pl.BlockSpec((1, d), lambda i: (0, 0)),
            pl.BlockSpec((npair, tt), lambda i: (0, i)),
            pl.BlockSpec(memory_space=pl.ANY),
        ],
        out_specs=pl.BlockSpec((tt, d), lambda i: (i, 0)),
        out_shape=jax.ShapeDtypeStruct((n, d), F32),
        scratch_shapes=[
            pltpu.SMEM((2 * tt * npair,), I32),
            pltpu.VMEM((tt, d), F32),
            pltpu.VMEM((tb, npair, d), jnp.uint32),
            pltpu.VMEM((tb, npair, d), jnp.uint32),
            pltpu.SemaphoreType.DMA((2,)),
            pltpu.SemaphoreType.DMA((2,)),
        ],
        compiler_params=_cparams("arbitrary"),
    )(ids, x, g.reshape(1, d).astype(F32), gates_t, uv_tab)


def _tile_gain(g, reps):
    return jnp.tile(g.astype(F32), reps)


def _layer(l, x, memn_kv, bias_a, bias_c, p):
    bsz, seq, d = x.shape
    n = bsz * seq
    lam_init = 0.8 - 0.6 * math.exp(-0.3 * l)
    x2 = x.reshape(n, d)
    w_in = p["w_in"][l]
    cols = np.cumsum([0, 512, 512, 512, 512, 512, 512, 128, 128, 512, 64, 8, 3 * d])
    seg = lambda a, b: w_in[:, cols[a]:cols[b]]
    gmix = p["norm_mix"][l]

    qk = _proj(x2, gmix, seg(0, 2), "norm64",
               jnp.concatenate([_tile_gain(p["a_q_gain"][l], 8), _tile_gain(p["a_k_gain"][l], 8)]), BF16)
    vv = _proj(x2, gmix, jnp.concatenate([seg(2, 3), seg(7, 8)], 1), "none", None, BF16, tn=640)
    u = _proj(x2, gmix, seg(3, 4), "gelu", None, F32)
    v = _proj(x2, gmix, seg(4, 5), "gelu_norm128", p["b_v_gain"][l], BF16)
    cqk = _proj(x2, gmix, seg(5, 7), "norm128",
                jnp.concatenate([_tile_gain(p["c_q_gain"][l], 4), p["c_k_gain"][l].astype(F32)]), BF16, tn=640)
    w_idx = jnp.concatenate([seg(8, 9), seg(9, 10), seg(9, 10), seg(10, 11),
                             jnp.zeros((d, 120), w_in.dtype)], 1)
    idx = _proj(x2, gmix, w_idx, "none", None, F32, tn=768)
    gates = _proj(x2, gmix, seg(11, 12), "sigmoid", None, F32)

    lamv = jnp.zeros((8, LANES), F32)
    for r, name in enumerate(("a_lq1", "a_lk1", "a_lq2", "a_lk2")):
        lamv = lamv.at[r, :A_DIM].set(p[name][l].astype(F32))
    ya = _diff_attn(qk.reshape(bsz, seq, -1), vv.reshape(bsz, seq, -1), lamv, bias_a,
                    p["a_subln_gain"][l], lam_init, bsz, seq)
    yb = _gmlp(u, v, p["b_w_s"][l], p["b_b_s"][l])
    k_sel = min(C_TOPK_MAX, seq // 4)
    yc = _dsa(cqk.reshape(bsz, seq, -1), vv.reshape(bsz, seq, -1), idx.reshape(bsz, seq, -1),
              bias_c, bsz, seq, k_sel)
    x2 = _merge(x2, ya.reshape(n, -1), yb, yc.reshape(n, -1), gates,
                p["w_br_a"][l], p["w_br_b"][l], p["w_br_c"][l], p["w_mix_out"][l])

    mk, mv = memn_kv
    mq = _proj(x2, p["norm_mem"][l], p["m_wq"][l], "norm128", _tile_gain(p["m_q_gain"][l], M_HEADS), BF16)
    x3 = _mem_attn(x2.reshape(bsz, seq, d), mq.reshape(bsz, seq, -1), mk, mv, p["m_wo"][l],
                   bsz, seq, mk.shape[1])
    x2 = x3.reshape(n, d)

    pq = _proj(x2, p["norm_peer"][l], p["p_wq"][l], "none", None, F32)
    ids_t, gates_t = _peer_topk(pq, p["p_subkey1"][l], p["p_subkey2"][l])
    x2 = _peer_expert(x2, p["norm_peer"][l], ids_t, gates_t, _pack_tables(p["p_u"][l], p["p_v"][l]))
    return x2.reshape(bsz, seq, d)


def kernel(x, mem, rel_bias, norm_mix, w_in, a_q_gain, a_k_gain, a_lq1, a_lk1, a_lq2, a_lk2, a_subln_gain, b_v_gain, b_w_s, b_b_s, c_q_gain, c_k_gain, w_br_a, w_br_b, w_br_c, w_mix_out, norm_mem, norm_memsrc, m_wq, m_wkv, m_q_gain, m_k_gain, m_wo, norm_peer, p_wq, p_subkey1, p_subkey2, p_u, p_v):
    p = dict(norm_mix=norm_mix, w_in=w_in, a_q_gain=a_q_gain, a_k_gain=a_k_gain, a_lq1=a_lq1,
             a_lk1=a_lk1, a_lq2=a_lq2, a_lk2=a_lk2, a_subln_gain=a_subln_gain, b_v_gain=b_v_gain,
             b_w_s=b_w_s, b_b_s=b_b_s, c_q_gain=c_q_gain, c_k_gain=c_k_gain, w_br_a=w_br_a,
             w_br_b=w_br_b, w_br_c=w_br_c, w_mix_out=w_mix_out, norm_mem=norm_mem, m_wq=m_wq,
             m_q_gain=m_q_gain, m_wo=m_wo, norm_peer=norm_peer, p_wq=p_wq, p_subkey1=p_subkey1,
             p_subkey2=p_subkey2, p_u=p_u, p_v=p_v)
    bsz, seq, d = x.shape
    mlen = mem.shape[1]
    depth = w_in.shape[0]
    bias_a = _bias_tiles(rel_bias, slice(0, A_HEADS), min(256, seq), seq)
    bias_c = _bias_tiles(rel_bias, slice(A_HEADS, A_HEADS + C_HEADS), LANES, seq)
    mem2 = mem.reshape(bsz * mlen, d)
    w = M_HEADS * M_DIM
    for l in range(depth):
        mk = _proj(mem2, norm_memsrc[l], m_wkv[l][:, :w], "norm128", _tile_gain(m_k_gain[l], M_HEADS), BF16)
        mv = _proj(mem2, norm_memsrc[l], m_wkv[l][:, w:], "none", None, BF16)
        x = _layer(l, x, (mk.reshape(bsz, mlen, w), mv.reshape(bsz, mlen, w)), bias_a, bias_c, p)
    return x
```

```python
import functools
import math

import jax
import jax.numpy as jnp
import numpy as np
from jax import lax
from jax.experimental import pallas as pl
from jax.experimental.pallas import tpu as pltpu

F32 = jnp.float32
BF16 = jnp.bfloat16
I32 = jnp.int32

EPS = 1e-6
NEG = -1e30
INT_MIN = -(2 ** 31)

A_HEADS = 4
A_DIM = 64
B_GROUPS = 4
B_GROUP_DIM = 128
B_CHUNK = 128
C_HEADS = 4
C_DIM = 128
C_IDX_HEADS = 8
C_IDX_DIM = 64
C_TOPK_MAX = 256
M_HEADS = 4
M_DIM = 128
P_HEADS = 8
P_QDIM = 256
P_NKEYS = 128
P_TOPK = 16
REL_BUCKETS = 32
REL_MAX_DIST = 128
LANES = 128

DSA_KEY_RANGES = 8
PEER_TOPK_HEADS_PER_STEP = 4

VMEM_LIMIT = 56 * 1024 * 1024


def _cparams(*sem):
    return pltpu.CompilerParams(dimension_semantics=sem, vmem_limit_bytes=VMEM_LIMIT)


def _gelu(x):
    return 0.5 * x * (1.0 + lax.erf(x * (1.0 / math.sqrt(2.0))))


def _group_norm(h, gain, group):
    parts = []
    for c in range(h.shape[-1] // LANES):
        hc = h[:, c * LANES:(c + 1) * LANES]
        sq = hc * hc
        if group == LANES:
            r = lax.rsqrt(jnp.sum(sq, -1, keepdims=True) * (1.0 / LANES) + EPS)
        else:
            lo = lax.broadcasted_iota(I32, hc.shape, 1) < 64
            s_lo = jnp.sum(jnp.where(lo, sq, 0.0), -1, keepdims=True)
            s_hi = jnp.sum(jnp.where(lo, 0.0, sq), -1, keepdims=True)
            r = jnp.where(lo, lax.rsqrt(s_lo * (1.0 / 64) + EPS), lax.rsqrt(s_hi * (1.0 / 64) + EPS))
        parts.append(hc * r)
    out = parts[0] if len(parts) == 1 else jnp.concatenate(parts, -1)
    return out * gain


def _epilogue(h, gain, mode):
    if mode == "none":
        return h
    if mode == "norm64":
        return _group_norm(h, gain, 64)
    if mode == "norm128":
        return _group_norm(h, gain, 128)
    if mode == "gelu":
        return _gelu(h)
    if mode == "gelu_norm128":
        return _group_norm(_gelu(h), gain, 128)
    if mode == "sigmoid":
        return jax.nn.sigmoid(h)
    raise ValueError(mode)


def _proj_kernel(x_ref, g_ref, w_ref, e_ref, o_ref, xn_ref, *, mode):
    @pl.when(pl.program_id(1) == 0)
    def _():
        x = x_ref[...]
        ms = jnp.mean(x * x, axis=-1, keepdims=True)
        xn_ref[...] = (x * lax.rsqrt(ms + EPS) * g_ref[...]).astype(BF16)

    h = jnp.dot(xn_ref[...], w_ref[...], preferred_element_type=F32)
    o_ref[...] = _epilogue(h, e_ref[...], mode).astype(o_ref.dtype)


def _proj(x, g, w, mode="none", gain=None, out_dtype=F32, tn=512):
    n, d = x.shape
    dout = w.shape[1]
    tm = min(1024, n)
    assert n % tm == 0 and dout % tn == 0, (n, dout, tn)
    if gain is None:
        gain = jnp.ones((dout,), F32)
    return pl.pallas_call(
        functools.partial(_proj_kernel, mode=mode),
        grid=(n // tm, dout // tn),
        in_specs=[
            pl.BlockSpec((tm, d), lambda i, j: (i, 0)),
            pl.BlockSpec((1, d), lambda i, j: (0, 0)),
            pl.BlockSpec((d, tn), lambda i, j: (0, j)),
            pl.BlockSpec((1, tn), lambda i, j: (0, j)),
        ],
        out_specs=pl.BlockSpec((tm, tn), lambda i, j: (i, j)),
        out_shape=jax.ShapeDtypeStruct((n, dout), out_dtype),
        scratch_shapes=[pltpu.VMEM((tm, d), BF16)],
        compiler_params=_cparams("parallel", "arbitrary"),
    )(x, g.reshape(1, d).astype(F32), w.astype(BF16), gain.reshape(1, dout).astype(F32))


def _t5_bucket(dist):
    n = jnp.maximum(dist, 0)
    max_exact = REL_BUCKETS // 2
    nf = jnp.maximum(n, 1).astype(F32)
    large = max_exact + (jnp.log(nf / max_exact) / math.log(REL_MAX_DIST / max_exact)
                         * (REL_BUCKETS - max_exact)).astype(I32)
    large = jnp.minimum(large, REL_BUCKETS - 1)
    return jnp.where(n < max_exact, n, large)


def _far_bucket_is_constant(t, seq):
    d = np.arange(t + 1, max(seq, t + 2), dtype=np.float64)
    max_exact = REL_BUCKETS // 2
    large = max_exact + np.floor(np.log(d / max_exact) / math.log(REL_MAX_DIST / max_exact)
                                 * (REL_BUCKETS - max_exact) + 1e-6).astype(np.int64)
    safe = max_exact + np.floor(np.log(d / max_exact) / math.log(REL_MAX_DIST / max_exact)
                                * (REL_BUCKETS - max_exact) - 1e-3).astype(np.int64)
    return bool(np.all(np.minimum(large, REL_BUCKETS - 1) == REL_BUCKETS - 1)
                and np.all(np.minimum(safe, REL_BUCKETS - 1) == REL_BUCKETS - 1))


def _bias_tiles(rel_bias, heads, t, seq):
    assert _far_bucket_is_constant(t, seq)
    tab = rel_bias[_t5_bucket(jnp.arange(2 * t, dtype=I32))][:, heads].astype(F32)
    i = jnp.arange(t, dtype=I32)[:, None]
    j = jnp.arange(t, dtype=I32)[None, :]
    d0 = i - j
    diag = jnp.where((d0 >= 0)[..., None], tab[jnp.maximum(d0, 0)], NEG)
    prev = tab[t + d0]
    far = rel_bias[REL_BUCKETS - 1, heads].astype(F32)
    nh = far.shape[0]
    return (jnp.transpose(diag, (2, 0, 1)), jnp.transpose(prev, (2, 0, 1)),
            jnp.broadcast_to(far[:, None, None], (nh, 1, LANES)))


def _diff_attn_kernel(lam_ref, q_ref, k_ref, v_ref, wd_ref, wp_ref, c_ref, g_ref, o_ref, *,
                      t, scale, lam_init):
    i = pl.program_id(1)
    lo = lax.broadcasted_iota(I32, (t, LANES), 1) < A_DIM
    qs = []
    for h in range(A_HEADS):
        q = q_ref[:, h * LANES:(h + 1) * LANES]
        zero = jnp.zeros_like(q)
        qs.append((jnp.where(lo, q, zero), jnp.where(lo, zero, q)))

    def tile(start, bias_of, carry):
        out = []
        for h in range(A_HEADS):
            k = k_ref[pl.ds(start, t), h * LANES:(h + 1) * LANES]
            v = v_ref[pl.ds(start, t), h * LANES:(h + 1) * LANES]
            bias = bias_of(h)
            for m in range(2):
                mo, lo_, ao = carry[2 * h + m]
                s = lax.dot_general(qs[h][m], k, (((1,), (1,)), ((), ())), preferred_element_type=F32)
                s = s * scale + bias
                mn = jnp.maximum(mo, jnp.max(s, -1, keepdims=True))
                a = jnp.exp(mo - mn)
                p = jnp.exp(s - mn)
                ln = a * lo_ + jnp.sum(p, -1, keepdims=True)
                an = a * ao + jnp.dot(p.astype(BF16), v, preferred_element_type=F32)
                out.append((mn, ln, an))
        return tuple(out)

    one = (jnp.full((t, 1), NEG, F32), jnp.zeros((t, 1), F32), jnp.zeros((t, LANES), F32))
    carry = lax.fori_loop(0, jnp.maximum(i - 1, 0),
                          lambda j, cr: tile(pl.multiple_of(j * t, t), lambda h: c_ref[h, 0:1, 0:1], cr),
                          (one,) * (2 * A_HEADS))
    jp = jnp.maximum(i - 1, 0)
    carry = tile(pl.multiple_of(jp * t, t), lambda h: jnp.where(i > 0, wp_ref[h], NEG), carry)
    carry = tile(pl.multiple_of(i * t, t), lambda h: wd_ref[h], carry)

    lv = lam_ref[...]
    lam = (jnp.exp(jnp.sum(lv[0:1] * lv[1:2], -1, keepdims=True))
           - jnp.exp(jnp.sum(lv[2:3] * lv[3:4], -1, keepdims=True)) + lam_init)
    for h in range(A_HEADS):
        c1, c2 = carry[2 * h], carry[2 * h + 1]
        o = c1[2] / c1[1] - lam * (c2[2] / c2[1])
        r = lax.rsqrt(jnp.mean(o * o, -1, keepdims=True) + EPS)
        o_ref[:, h * LANES:(h + 1) * LANES] = ((o * r * g_ref[...]) * (1.0 - lam_init)).astype(o_ref.dtype)


def _diff_attn(qk, vv, lamv, bias, gain, lam_init, bsz, seq):
    t = min(256, seq)
    nq = seq // t
    w = A_HEADS * LANES
    wd, wp, far = bias
    return pl.pallas_call(
        functools.partial(_diff_attn_kernel, t=t, scale=A_DIM ** -0.5, lam_init=lam_init),
        grid=(bsz, nq),
        in_specs=[
            pl.BlockSpec((8, LANES), lambda b, i: (0, 0)),
            pl.BlockSpec((None, t, w), lambda b, i: (b, i, 0)),
            pl.BlockSpec((None, seq, w), lambda b, i: (b, 0, 1)),
            pl.BlockSpec((None, seq, vv.shape[-1]), lambda b, i: (b, 0, 0)),
            pl.BlockSpec((A_HEADS, t, t), lambda b, i: (0, 0, 0)),
            pl.BlockSpec((A_HEADS, t, t), lambda b, i: (0, 0, 0)),
            pl.BlockSpec((A_HEADS, 1, LANES), lambda b, i: (0, 0, 0)),
            pl.BlockSpec((1, LANES), lambda b, i: (0, 0)),
        ],
        out_specs=pl.BlockSpec((None, t, w), lambda b, i: (b, i, 0)),
        out_shape=jax.ShapeDtypeStruct((bsz, seq, w), BF16),
        compiler_params=_cparams("parallel", "arbitrary"),
    )(lamv, qk, qk, vv, wd, wp, far, gain.reshape(1, LANES).astype(F32))


def _gmlp_kernel(u_ref, v_ref, w_ref, b_ref, o_ref, *, chunks):
    row = lax.broadcasted_iota(I32, (B_CHUNK, B_CHUNK), 0)
    col = lax.broadcasted_iota(I32, (B_CHUNK, B_CHUNK), 1)
    ws = [jnp.where(row >= col, w_ref[g], 0.0).astype(BF16) for g in range(B_GROUPS)]
    for c in range(chunks):
        rs = slice(c * B_CHUNK, (c + 1) * B_CHUNK)
        for g in range(B_GROUPS):
            cs = slice(g * B_GROUP_DIM, (g + 1) * B_GROUP_DIM)
            sv = jnp.dot(ws[g], v_ref[rs, cs], preferred_element_type=F32) + b_ref[:, cs]
            o_ref[rs, cs] = (u_ref[rs, cs] * sv).astype(o_ref.dtype)


def _gmlp(u, v, w_s, b_s):
    n, width = u.shape
    chunks = 4
    tm = chunks * B_CHUNK
    bfull = jnp.repeat(b_s.T.astype(F32), B_GROUP_DIM, axis=1)
    return pl.pallas_call(
        functools.partial(_gmlp_kernel, chunks=chunks),
        grid=(n // tm,),
        in_specs=[
            pl.BlockSpec((tm, width), lambda i: (i, 0)),
            pl.BlockSpec((tm, width), lambda i: (i, 0)),
            pl.BlockSpec((B_GROUPS, B_CHUNK, B_CHUNK), lambda i: (0, 0, 0)),
            pl.BlockSpec((B_CHUNK, width), lambda i: (0, 0)),
        ],
        out_specs=pl.BlockSpec((tm, width), lambda i: (i, 0)),
        out_shape=jax.ShapeDtypeStruct((n, width), BF16),
        compiler_params=_cparams("parallel"),
    )(u, v, w_s.astype(F32), bfull)


def _dsa_kernel(cq_ref, ck_ref, cv_ref, iq_ref, ikk_ref, iwq_ref, wd_ref, wp_ref, c_ref, o_ref,
                key_ref, sel_ref, *, t, seq, k_sel, scale, q0):
    i = pl.program_id(1) + q0
    nt = seq // t
    qpos = i * t + lax.broadcasted_iota(I32, (t, seq), 0)
    kpos = lax.broadcasted_iota(I32, (t, seq), 1)
    valid = kpos <= qpos

    ikk = ikk_ref[...].astype(BF16)
    iw = iwq_ref[...]
    score = jnp.zeros((t, seq), F32)
    for pair in range(C_IDX_HEADS // 2):
        qp = iq_ref[:, pair * LANES:(pair + 1) * LANES].astype(BF16)
        lo = lax.broadcasted_iota(I32, qp.shape, 1) < C_IDX_DIM
        zero = jnp.zeros_like(qp)
        for half in range(2):
            h = 2 * pair + half
            qh = jnp.where(lo, qp, zero) if half == 0 else jnp.where(lo, zero, qp)
            d = lax.dot_general(qh, ikk, (((1,), (1,)), ((), ())), preferred_element_type=F32)
            score = score + jnp.maximum(d, 0.0) * iw[:, h:h + 1]

    bits = pltpu.bitcast(score + 0.0, I32)
    skey = bits ^ ((bits >> 31) & 0x7FFFFFFF)
    key_ref[...] = jnp.where(valid, skey, INT_MIN)

    def count(mask):
        return jnp.sum(jnp.where(mask, 1.0, 0.0), axis=-1, keepdims=True)

    def vbody(it, p_u):
        cand = p_u | (jnp.int32(1) << (31 - it))
        cnt = count(key_ref[...] >= (cand ^ INT_MIN))
        return jnp.where(cnt >= k_sel, cand, p_u)

    p_u = lax.fori_loop(0, 32, vbody, jnp.zeros((t, 1), I32))
    thr = p_u ^ INT_MIN
    keys = key_ref[...]
    gt = keys > thr
    eq = (keys == thr) & valid
    need = k_sel - count(gt)
    sel_ref[...] = jnp.where(valid & (keys >= thr), 1.0, 0.0)

    nbits = max(1, (seq - 1).bit_length())

    @pl.when(jnp.max(count(eq) - need) > 0.0)
    def _():
        def ibody(it, m):
            cand = m | (jnp.int32(1) << (nbits - 1 - it))
            cnt = count(eq & (kpos < cand))
            return jnp.where(cnt < need, cand, m)

        m = lax.fori_loop(0, nbits, ibody, jnp.zeros((t, 1), I32))
        sel_ref[...] = jnp.where(valid & (gt | (eq & (kpos <= m))), 1.0, 0.0)

    sel = sel_ref[...] > 0.5

    ck = ck_ref[...]
    cv = cv_ref[...]
    for h in range(C_HEADS):
        far = c_ref[h, 0:1, 0:1]
        tiles = []
        for jt in range(nt):
            tiles.append(jnp.where(i == jt, wd_ref[h], jnp.where(i == jt + 1, wp_ref[h], far)))
        bias = tiles[0] if nt == 1 else jnp.concatenate(tiles, -1)
        s = lax.dot_general(cq_ref[:, h * C_DIM:(h + 1) * C_DIM], ck, (((1,), (1,)), ((), ())),
                            preferred_element_type=F32)
        s = jnp.where(sel, s * scale + bias, NEG)
        p = jnp.exp(s - jnp.max(s, -1, keepdims=True))
        p = p / jnp.sum(p, -1, keepdims=True)
        o_ref[:, h * C_DIM:(h + 1) * C_DIM] = jnp.dot(
            p.astype(BF16), cv, preferred_element_type=F32).astype(o_ref.dtype)


def _dsa(cqk, vv, idx, bias, bsz, seq, k_sel):
    t = LANES
    wd, wp, far = bias
    nq = seq // t
    nrange = math.gcd(nq, DSA_KEY_RANGES)
    per = nq // nrange
    outs = []
    for r in range(nrange):
        q0 = r * per
        sk = (r + 1) * per * t
        qmap = lambda b, i, q0=q0: (b, i + q0, 0)
        outs.append(pl.pallas_call(
            functools.partial(_dsa_kernel, t=t, seq=sk, k_sel=k_sel, scale=C_DIM ** -0.5, q0=q0),
            grid=(bsz, per),
            in_specs=[
                pl.BlockSpec((None, t, C_HEADS * C_DIM), qmap),
                pl.BlockSpec((None, sk, C_DIM), lambda b, i: (b, 0, C_HEADS)),
                pl.BlockSpec((None, sk, C_DIM), lambda b, i: (b, 0, C_HEADS)),
                pl.BlockSpec((None, t, C_IDX_HEADS * C_IDX_DIM), qmap),
                pl.BlockSpec((None, sk, LANES), lambda b, i: (b, 0, 4)),
                pl.BlockSpec((None, t, LANES), lambda b, i, q0=q0: (b, i + q0, 5)),
                pl.BlockSpec((C_HEADS, t, t), lambda b, i: (0, 0, 0)),
                pl.BlockSpec((C_HEADS, t, t), lambda b, i: (0, 0, 0)),
                pl.BlockSpec((C_HEADS, 1, LANES), lambda b, i: (0, 0, 0)),
            ],
            out_specs=pl.BlockSpec((None, t, C_HEADS * C_DIM), lambda b, i: (b, i, 0)),
            out_shape=jax.ShapeDtypeStruct((bsz, per * t, C_HEADS * C_DIM), BF16),
            scratch_shapes=[pltpu.VMEM((t, sk), I32), pltpu.VMEM((t, sk), F32)],
            compiler_params=_cparams("parallel", "arbitrary"),
        )(cqk, cqk, vv, idx, idx, idx, wd, wp, far))
    return outs[0] if nrange == 1 else jnp.concatenate(outs, axis=1)


def _merge_kernel(x_ref, ya_ref, yb_ref, yc_ref, g_ref, wa_ref, wb_ref, wc_ref, wo_ref, o_ref, *, d):
    merged = (g_ref[:, 0:d] * jnp.dot(ya_ref[...], wa_ref[...], preferred_element_type=F32)
              + g_ref[:, d:2 * d] * jnp.dot(yb_ref[...], wb_ref[...], preferred_element_type=F32)
              + g_ref[:, 2 * d:3 * d] * jnp.dot(yc_ref[...], wc_ref[...], preferred_element_type=F32))
    o_ref[...] = x_ref[...] + jnp.dot(merged.astype(BF16), wo_ref[...], preferred_element_type=F32)


def _merge(x, ya, yb, yc, gates, wa, wb, wc, wo):
    n, d = x.shape
    bw = ya.shape[1]
    tm = min(512, n)
    row = lambda w: pl.BlockSpec((tm, w), lambda i: (i, 0))
    full = lambda a, b: pl.BlockSpec((a, b), lambda i: (0, 0))
    return pl.pallas_call(
        functools.partial(_merge_kernel, d=d),
        grid=(n // tm,),
        in_specs=[row(d), row(bw), row(bw), row(bw), row(3 * d),
                  full(bw, d), full(bw, d), full(bw, d), full(d, d)],
        out_specs=row(d),
        out_shape=jax.ShapeDtypeStruct((n, d), F32),
        compiler_params=_cparams("parallel"),
    )(x, ya, yb, yc, gates, wa.astype(BF16), wb.astype(BF16), wc.astype(BF16), wo.astype(BF16))


def _mem_attn_kernel(x_ref, q_ref, k_ref, v_ref, wo_ref, o_ref, *, scale):
    outs = []
    for h in range(M_HEADS):
        cs = slice(h * M_DIM, (h + 1) * M_DIM)
        s = lax.dot_general(q_ref[:, cs], k_ref[:, cs], (((1,), (1,)), ((), ())),
                            preferred_element_type=F32) * scale
        p = jnp.exp(s - jnp.max(s, -1, keepdims=True))
        p = p / jnp.sum(p, -1, keepdims=True)
        outs.append(jnp.dot(p.astype(BF16), v_ref[:, cs], preferred_element_type=F32).astype(BF16))
    o = jnp.concatenate(outs, -1)
    o_ref[...] = x_ref[...] + jnp.dot(o, wo_ref[...], preferred_element_type=F32)


def _mem_attn(x, q, k, v, wo, bsz, seq, mlen):
    d = x.shape[-1]
    w = M_HEADS * M_DIM
    t = min(512, seq)
    return pl.pallas_call(
        functools.partial(_mem_attn_kernel, scale=M_DIM ** -0.5),
        grid=(bsz, seq // t),
        in_specs=[
            pl.BlockSpec((None, t, d), lambda b, i: (b, i, 0)),
            pl.BlockSpec((None, t, w), lambda b, i: (b, i, 0)),
            pl.BlockSpec((None, mlen, w), lambda b, i: (b, 0, 0)),
            pl.BlockSpec((None, mlen, w), lambda b, i: (b, 0, 0)),
            pl.BlockSpec((w, d), lambda b, i: (0, 0)),
        ],
        out_specs=pl.BlockSpec((None, t, d), lambda b, i: (b, i, 0)),
        out_shape=jax.ShapeDtypeStruct((bsz, seq, d), F32),
        compiler_params=_cparams("parallel", "arbitrary"),
    )(x, q, k, v, wo.astype(BF16))


def _topk_rows_ids(arr, ids, k):
    vals, idxs = [], []
    for _ in range(k):
        m = jnp.max(arr, axis=0, keepdims=True)
        am = jnp.min(jnp.where(arr == m, ids, 2 ** 30), axis=0, keepdims=True)
        vals.append(m)
        idxs.append(am)
        arr = jnp.where(ids == am, -jnp.inf, arr)
    return jnp.concatenate(vals, 0), jnp.concatenate(idxs, 0)


def _topk_rows(arr, k):
    return _topk_rows_ids(arr, lax.broadcasted_iota(I32, arr.shape, 0), k)


def _topk_pair_sums(v1, v2):
    k = P_TOPK
    t = v1.shape[1]
    r8 = lax.broadcasted_iota(I32, (8, t), 0)
    r16 = lax.broadcasted_iota(I32, (16, t), 0)
    vals = [v1[0:1] + v2]
    ids = [r16]
    for a in (1, 2, 3):
        vals.append(v1[a:a + 1] + v2[0:8])
        ids.append(a * k + r8)
    for b in (0, 1, 2):
        vals.append(jnp.where(r8 >= 4, v1[0:8] + v2[b:b + 1], -jnp.inf))
        ids.append(r8 * k + b)
    vals.append(v1[8:16] + v2[0:1])
    ids.append((r8 + 8) * k)
    return _topk_rows_ids(jnp.concatenate(vals, 0), jnp.concatenate(ids, 0), k)


def _pick_rows(table, sel):
    out = jnp.zeros(sel.shape, table.dtype)
    for r in range(table.shape[0]):
        out = jnp.where(sel == r, table[r:r + 1, :], out)
    return out


def _peer_topk_kernel(q_ref, k1_ref, k2_ref, ids_ref, gate_ref, *, heads):
    half = P_QDIM // 2
    dn = (((1,), (1,)), ((), ()))
    for h in range(heads):
        q1 = q_ref[:, h * P_QDIM:h * P_QDIM + half].astype(BF16)
        q2 = q_ref[:, h * P_QDIM + half:(h + 1) * P_QDIM].astype(BF16)
        s1 = lax.dot_general(k1_ref[...], q1, dn, preferred_element_type=F32)
        s2 = lax.dot_general(k2_ref[...], q2, dn, preferred_element_type=F32)
        v1, i1 = _topk_rows(s1, P_TOPK)
        v2, i2 = _topk_rows(s2, P_TOPK)
        cs, ci = _topk_pair_sums(v1, v2)
        e1 = _pick_rows(i1, ci >> 4)
        e2 = _pick_rows(i2, ci & (P_TOPK - 1))
        rows = slice(h * P_TOPK, (h + 1) * P_TOPK)
        ids_ref[rows, :] = e1 * P_NKEYS + e2
        p = jnp.exp(cs - jnp.max(cs, axis=0, keepdims=True))
        gate_ref[rows, :] = p / jnp.sum(p, axis=0, keepdims=True)


def _peer_topk(q, sk1, sk2):
    n = q.shape[0]
    t = LANES
    assert P_TOPK == 16
    hb = PEER_TOPK_HEADS_PER_STEP
    return pl.pallas_call(
        functools.partial(_peer_topk_kernel, heads=hb),
        grid=(n // t, P_HEADS // hb),
        in_specs=[
            pl.BlockSpec((t, hb * P_QDIM), lambda i, h: (i, h)),
            pl.BlockSpec((P_NKEYS, P_QDIM // 2), lambda i, h: (0, 0)),
            pl.BlockSpec((P_NKEYS, P_QDIM // 2), lambda i, h: (0, 0)),
        ],
        out_specs=[
            pl.BlockSpec((hb * P_TOPK, t), lambda i, h: (h, i)),
            pl.BlockSpec((hb * P_TOPK, t), lambda i, h: (h, i)),
        ],
        out_shape=[
            jax.ShapeDtypeStruct((P_HEADS * P_TOPK, n), I32),
            jax.ShapeDtypeStruct((P_HEADS * P_TOPK, n), F32),
        ],
        compiler_params=_cparams("parallel", "arbitrary"),
    )(q, sk1.astype(BF16), sk2.astype(BF16))


PEER_TOKENS_PER_STEP = 128
PEER_TOKENS_PER_SLOT = 8


def _pack_tables(u_tab, v_tab):
    ub = lax.bitcast_convert_type(u_tab.astype(BF16), jnp.uint16).astype(jnp.uint32)
    vb = lax.bitcast_convert_type(v_tab.astype(BF16), jnp.uint16).astype(jnp.uint32)
    return (ub | (vb << 16))[:, None, :]


def _peer_expert_kernel(ids_hbm, x_ref, g_ref, gate_ref, uv_hbm, o_ref,
                        ids_smem, xn_ref, buf0, buf1, sem_ids, sem_row, *, tt, tb, npair, nsteps):
    i = pl.program_id(0)
    rows = tb * npair
    nsub = tt // tb
    m = tt * npair
    bufs = (buf0, buf1)
    cur = i & 1
    has_next = i + 1 < nsteps

    def ids_copy(step, half):
        return pltpu.make_async_copy(ids_hbm.at[step], ids_smem.at[pl.ds(half * m, m)], sem_ids.at[half])

    def issue_token(ids_base, j, dst, slot):
        tok_base = ids_base + j * npair
        for r in range(npair):
            e = ids_smem[tok_base + r]
            pltpu.make_async_copy(uv_hbm.at[e], dst.at[j, pl.ds(r, 1)],
                                  sem_row.at[slot]).start(priority=r % 2)

    def wait_rows(slot):
        pltpu.make_async_copy(bufs[1 - slot], bufs[slot], sem_row.at[slot]).wait()

    @pl.when(i == 0)
    def _():
        first = ids_copy(0, 0)
        first.start()
        first.wait()

        def body(j, carry):
            issue_token(0, j, buf0, 0)
            return carry
        lax.fori_loop(0, tb, body, 0)

    @pl.when(has_next)
    def _():
        ids_copy(i + 1, 1 - cur).start()

    x = x_ref[...]
    ms = jnp.mean(x * x, axis=-1, keepdims=True)
    xn_ref[...] = x * lax.rsqrt(ms + EPS) * g_ref[...]
    rowid = lax.broadcasted_iota(I32, (tb, npair), 0)

    def sub_batch(sb, slot):
        src, dst = bufs[slot], bufs[1 - slot]
        wait_rows(slot)
        is_last = sb == nsub - 1

        @pl.when(jnp.logical_and(is_last, has_next))
        def _():
            ids_copy(i + 1, 1 - cur).wait()

        nxt_half = jnp.where(jnp.logical_and(is_last, has_next), 1 - cur, cur)
        nxt_sb = jnp.where(is_last, jnp.where(has_next, 0, sb), sb + 1)
        ids_base = nxt_half * m + nxt_sb * rows

        t0 = pl.multiple_of(sb * tb, tb)
        xs = xn_ref[pl.ds(t0, tb), :].astype(BF16)
        gates = gate_ref[pl.ds(t0, tb), :]
        acc = jnp.zeros((tb, x_ref.shape[1]), F32)
        for j in range(tb):
            issue_token(ids_base, j, dst, 1 - slot)
            w = src[j]
            u = pltpu.bitcast(w << 16, F32).astype(BF16)
            v = pltpu.bitcast(w & jnp.uint32(0xFFFF0000), F32).astype(BF16)
            h = lax.dot_general(xs, u, (((1,), (1,)), ((), ())), preferred_element_type=F32)
            coef = jnp.where(rowid == j, gates * _gelu(h), 0.0).astype(BF16)
            acc = acc + jnp.dot(coef, v, preferred_element_type=F32)
        o_ref[pl.ds(t0, tb), :] = x_ref[pl.ds(t0, tb), :] + acc

    def pair(sp, carry):
        sub_batch(2 * sp, 0)
        sub_batch(2 * sp + 1, 1)
        return carry

    lax.fori_loop(0, nsub // 2, pair, 0)

    @pl.when(i == nsteps - 1)
    def _():
        wait_rows(0)


def _peer_expert(x, g, ids_t, gates_t, uv_tab):
    n, d = x.shape
    npair = ids_t.shape[0]
    tt, tb = PEER_TOKENS_PER_STEP, PEER_TOKENS_PER_SLOT
    assert n % tt == 0 and tt % (2 * tb) == 0
    nsteps = n // tt
    ids = ids_t.T.reshape(nsteps, tt * npair)
    return pl.pallas_call(
        functools.partial(_peer_expert_kernel, tt=tt, tb=tb, npair=npair, nsteps=nsteps),
        grid=(nsteps,),
        in_specs=[
            pl.BlockSpec(memory_space=pl.ANY),
            pl.BlockSpec((tt, d), lambda i: (i, 0)),
            pl.BlockSpec((1, d), lambda i: (0, 0)),
            pl.BlockSpec((tt, npair), lambda i: (i, 0)),
            pl.BlockSpec(memory_space=pl.ANY),
        ],
        out_specs=pl.BlockSpec((tt, d), lambda i: (i, 0)),
        out_shape=jax.ShapeDtypeStruct((n, d), F32),
        scratch_shapes=[
            pltpu.SMEM((2 * tt * npair,), I32),
            pltpu.VMEM((tt, d), F32),
            pltpu.VMEM((tb, npair, d), jnp.uint32),
            pltpu.VMEM((tb, npair, d), jnp.uint32),
            pltpu.SemaphoreType.DMA((2,)),
            pltpu.SemaphoreType.DMA((2,)),
        ],
        compiler_params=_cparams("arbitrary"),
    )(ids, x, g.reshape(1, d).astype(F32), gates_t.T, uv_tab)


def _tile_gain(g, reps):
    return jnp.tile(g.astype(F32), reps)


def _layer(l, x, memn_kv, bias_a, bias_c, p):
    bsz, seq, d = x.shape
    n = bsz * seq
    lam_init = 0.8 - 0.6 * math.exp(-0.3 * l)
    x2 = x.reshape(n, d)
    w_in = p["w_in"][l]
    cols = np.cumsum([0, 512, 512, 512, 512, 512, 512, 128, 128, 512, 64, 8, 3 * d])
    seg = lambda a, b: w_in[:, cols[a]:cols[b]]
    gmix = p["norm_mix"][l]

    qk = _proj(x2, gmix, seg(0, 2), "norm64",
               jnp.concatenate([_tile_gain(p["a_q_gain"][l], 8), _tile_gain(p["a_k_gain"][l], 8)]), BF16)
    vv = _proj(x2, gmix, jnp.concatenate([seg(2, 3), seg(7, 8)], 1), "none", None, BF16, tn=640)
    u = _proj(x2, gmix, seg(3, 4), "gelu", None, F32)
    v = _proj(x2, gmix, seg(4, 5), "gelu_norm128", p["b_v_gain"][l], BF16)
    cqk = _proj(x2, gmix, seg(5, 7), "norm128",
                jnp.concatenate([_tile_gain(p["c_q_gain"][l], 4), p["c_k_gain"][l].astype(F32)]), BF16, tn=640)
    w_idx = jnp.concatenate([seg(8, 9), seg(9, 10), seg(9, 10), seg(10, 11),
                             jnp.zeros((d, 120), w_in.dtype)], 1)
    idx = _proj(x2, gmix, w_idx, "none", None, F32, tn=768)
    gates = _proj(x2, gmix, seg(11, 12), "sigmoid", None, F32)

    lamv = jnp.zeros((8, LANES), F32)
    for r, name in enumerate(("a_lq1", "a_lk1", "a_lq2", "a_lk2")):
        lamv = lamv.at[r, :A_DIM].set(p[name][l].astype(F32))
    ya = _diff_attn(qk.reshape(bsz, seq, -1), vv.reshape(bsz, seq, -1), lamv, bias_a,
                    p["a_subln_gain"][l], lam_init, bsz, seq)
    yb = _gmlp(u, v, p["b_w_s"][l], p["b_b_s"][l])
    k_sel = min(C_TOPK_MAX, seq // 4)
    yc = _dsa(cqk.reshape(bsz, seq, -1), vv.reshape(bsz, seq, -1), idx.reshape(bsz, seq, -1),
              bias_c, bsz, seq, k_sel)
    x2 = _merge(x2, ya.reshape(n, -1), yb, yc.reshape(n, -1), gates,
                p["w_br_a"][l], p["w_br_b"][l], p["w_br_c"][l], p["w_mix_out"][l])

    mk, mv = memn_kv
    mq = _proj(x2, p["norm_mem"][l], p["m_wq"][l], "norm128", _tile_gain(p["m_q_gain"][l], M_HEADS), BF16)
    x3 = _mem_attn(x2.reshape(bsz, seq, d), mq.reshape(bsz, seq, -1), mk, mv, p["m_wo"][l],
                   bsz, seq, mk.shape[1])
    x2 = x3.reshape(n, d)

    pq = _proj(x2, p["norm_peer"][l], p["p_wq"][l], "none", None, F32)
    ids_t, gates_t = _peer_topk(pq, p["p_subkey1"][l], p["p_subkey2"][l])
    x2 = _peer_expert(x2, p["norm_peer"][l], ids_t, gates_t, _pack_tables(p["p_u"][l], p["p_v"][l]))
    return x2.reshape(bsz, seq, d)


def kernel(x, mem, rel_bias, norm_mix, w_in, a_q_gain, a_k_gain, a_lq1, a_lk1, a_lq2, a_lk2, a_subln_gain, b_v_gain, b_w_s, b_b_s, c_q_gain, c_k_gain, w_br_a, w_br_b, w_br_c, w_mix_out, norm_mem, norm_memsrc, m_wq, m_wkv, m_q_gain, m_k_gain, m_wo, norm_peer, p_wq, p_subkey1, p_subkey2, p_u, p_v):
    p = dict(norm_mix=norm_mix, w_in=w_in, a_q_gain=a_q_gain, a_k_gain=a_k_gain, a_lq1=a_lq1,
             a_lk1=a_lk1, a_lq2=a_lq2, a_lk2=a_lk2, a_subln_gain=a_subln_gain, b_v_gain=b_v_gain,
             b_w_s=b_w_s, b_b_s=b_b_s, c_q_gain=c_q_gain, c_k_gain=c_k_gain, w_br_a=w_br_a,
             w_br_b=w_br_b, w_br_c=w_br_c, w_mix_out=w_mix_out, norm_mem=norm_mem, m_wq=m_wq,
             m_q_gain=m_q_gain, m_wo=m_wo, norm_peer=norm_peer, p_wq=p_wq, p_subkey1=p_subkey1,
             p_subkey2=p_subkey2, p_u=p_u, p_v=p_v)
    bsz, seq, d = x.shape
    mlen = mem.shape[1]
    depth = w_in.shape[0]
    bias_a = _bias_tiles(rel_bias, slice(0, A_HEADS), min(256, seq), seq)
    bias_c = _bias_tiles(rel_bias, slice(A_HEADS, A_HEADS + C_HEADS), LANES, seq)
    mem2 = mem.reshape(bsz * mlen, d)
    w = M_HEADS * M_DIM
    for l in range(depth):
        mk = _proj(mem2, norm_memsrc[l], m_wkv[l][:, :w], "norm128", _tile_gain(m_k_gain[l], M_HEADS), BF16)
        mv = _proj(mem2, norm_memsrc[l], m_wkv[l][:, w:], "none", None, BF16)
        x = _layer(l, x, (mk.reshape(bsz, mlen, w), mv.reshape(bsz, mlen, w)), bias_a, bias_c, p)
    return x
```

```python
import functools
import math

import jax
import jax.numpy as jnp
import numpy as np
from jax import lax
from jax.experimental import pallas as pl
from jax.experimental.pallas import tpu as pltpu

F32 = jnp.float32
BF16 = jnp.bfloat16
I32 = jnp.int32

EPS = 1e-6
NEG = -1e30
INT_MIN = -(2 ** 31)

A_HEADS = 4
A_DIM = 64
B_GROUPS = 4
B_GROUP_DIM = 128
B_CHUNK = 128
C_HEADS = 4
C_DIM = 128
C_IDX_HEADS = 8
C_IDX_DIM = 64
C_TOPK_MAX = 256
M_HEADS = 4
M_DIM = 128
P_HEADS = 8
P_QDIM = 256
P_NKEYS = 128
P_TOPK = 16
REL_BUCKETS = 32
REL_MAX_DIST = 128
LANES = 128

DSA_KEY_RANGES = 8
PEER_TOPK_HEADS_PER_STEP = 4

VMEM_LIMIT = 56 * 1024 * 1024


def _cparams(*sem):
    return pltpu.CompilerParams(dimension_semantics=sem, vmem_limit_bytes=VMEM_LIMIT)


def _gelu(x):
    return 0.5 * x * (1.0 + lax.erf(x * (1.0 / math.sqrt(2.0))))


def _group_norm(h, gain, group):
    parts = []
    for c in range(h.shape[-1] // LANES):
        hc = h[:, c * LANES:(c + 1) * LANES]
        sq = hc * hc
        if group == LANES:
            r = lax.rsqrt(jnp.sum(sq, -1, keepdims=True) * (1.0 / LANES) + EPS)
        else:
            lo = lax.broadcasted_iota(I32, hc.shape, 1) < 64
            s_lo = jnp.sum(jnp.where(lo, sq, 0.0), -1, keepdims=True)
            s_hi = jnp.sum(jnp.where(lo, 0.0, sq), -1, keepdims=True)
            r = jnp.where(lo, lax.rsqrt(s_lo * (1.0 / 64) + EPS), lax.rsqrt(s_hi * (1.0 / 64) + EPS))
        parts.append(hc * r)
    out = parts[0] if len(parts) == 1 else jnp.concatenate(parts, -1)
    return out * gain


def _epilogue(h, gain, mode):
    if mode == "none":
        return h
    if mode == "norm64":
        return _group_norm(h, gain, 64)
    if mode == "norm128":
        return _group_norm(h, gain, 128)
    if mode == "gelu":
        return _gelu(h)
    if mode == "gelu_norm128":
        return _group_norm(_gelu(h), gain, 128)
    if mode == "sigmoid":
        return jax.nn.sigmoid(h)
    raise ValueError(mode)


def _proj_kernel(x_ref, g_ref, w_ref, e_ref, o_ref, xn_ref, *, mode):
    @pl.when(pl.program_id(1) == 0)
    def _():
        x = x_ref[...]
        ms = jnp.mean(x * x, axis=-1, keepdims=True)
        xn_ref[...] = (x * lax.rsqrt(ms + EPS) * g_ref[...]).astype(BF16)

    h = jnp.dot(xn_ref[...], w_ref[...], preferred_element_type=F32)
    o_ref[...] = _epilogue(h, e_ref[...], mode).astype(o_ref.dtype)


def _proj(x, g, w, mode="none", gain=None, out_dtype=F32, tn=512):
    n, d = x.shape
    dout = w.shape[1]
    tm = min(1024, n)
    assert n % tm == 0 and dout % tn == 0, (n, dout, tn)
    if gain is None:
        gain = jnp.ones((dout,), F32)
    return pl.pallas_call(
        functools.partial(_proj_kernel, mode=mode),
        grid=(n // tm, dout // tn),
        in_specs=[
            pl.BlockSpec((tm, d), lambda i, j: (i, 0)),
            pl.BlockSpec((1, d), lambda i, j: (0, 0)),
            pl.BlockSpec((d, tn), lambda i, j: (0, j)),
            pl.BlockSpec((1, tn), lambda i, j: (0, j)),
        ],
        out_specs=pl.BlockSpec((tm, tn), lambda i, j: (i, j)),
        out_shape=jax.ShapeDtypeStruct((n, dout), out_dtype),
        scratch_shapes=[pltpu.VMEM((tm, d), BF16)],
        compiler_params=_cparams("parallel", "arbitrary"),
    )(x, g.reshape(1, d).astype(F32), w.astype(BF16), gain.reshape(1, dout).astype(F32))


def _t5_bucket(dist):
    n = jnp.maximum(dist, 0)
    max_exact = REL_BUCKETS // 2
    nf = jnp.maximum(n, 1).astype(F32)
    large = max_exact + (jnp.log(nf / max_exact) / math.log(REL_MAX_DIST / max_exact)
                         * (REL_BUCKETS - max_exact)).astype(I32)
    large = jnp.minimum(large, REL_BUCKETS - 1)
    return jnp.where(n < max_exact, n, large)


def _far_bucket_is_constant(t, seq):
    d = np.arange(t + 1, max(seq, t + 2), dtype=np.float64)
    max_exact = REL_BUCKETS // 2
    large = max_exact + np.floor(np.log(d / max_exact) / math.log(REL_MAX_DIST / max_exact)
                                 * (REL_BUCKETS - max_exact) + 1e-6).astype(np.int64)
    safe = max_exact + np.floor(np.log(d / max_exact) / math.log(REL_MAX_DIST / max_exact)
                                * (REL_BUCKETS - max_exact) - 1e-3).astype(np.int64)
    return bool(np.all(np.minimum(large, REL_BUCKETS - 1) == REL_BUCKETS - 1)
                and np.all(np.minimum(safe, REL_BUCKETS - 1) == REL_BUCKETS - 1))


def _bias_tiles(rel_bias, heads, t, seq):
    assert _far_bucket_is_constant(t, seq)
    tab = rel_bias[_t5_bucket(jnp.arange(2 * t, dtype=I32))][:, heads].astype(F32)
    i = jnp.arange(t, dtype=I32)[:, None]
    j = jnp.arange(t, dtype=I32)[None, :]
    d0 = i - j
    diag = jnp.where((d0 >= 0)[..., None], tab[jnp.maximum(d0, 0)], NEG)
    prev = tab[t + d0]
    far = rel_bias[REL_BUCKETS - 1, heads].astype(F32)
    nh = far.shape[0]
    return (jnp.transpose(diag, (2, 0, 1)), jnp.transpose(prev, (2, 0, 1)),
            jnp.broadcast_to(far[:, None, None], (nh, 1, LANES)))


def _diff_attn_kernel(lam_ref, q_ref, k_ref, v_ref, wd_ref, wp_ref, c_ref, g_ref, o_ref, *,
                      t, scale, lam_init):
    i = pl.program_id(1)
    lo = lax.broadcasted_iota(I32, (t, LANES), 1) < A_DIM
    qs = []
    for h in range(A_HEADS):
        q = q_ref[:, h * LANES:(h + 1) * LANES]
        zero = jnp.zeros_like(q)
        qs.append((jnp.where(lo, q, zero), jnp.where(lo, zero, q)))

    def tile(start, bias_of, carry):
        out = []
        for h in range(A_HEADS):
            k = k_ref[pl.ds(start, t), h * LANES:(h + 1) * LANES]
            v = v_ref[pl.ds(start, t), h * LANES:(h + 1) * LANES]
            bias = bias_of(h)
            for m in range(2):
                mo, lo_, ao = carry[2 * h + m]
                s = lax.dot_general(qs[h][m], k, (((1,), (1,)), ((), ())), preferred_element_type=F32)
                s = s * scale + bias
                mn = jnp.maximum(mo, jnp.max(s, -1, keepdims=True))
                a = jnp.exp(mo - mn)
                p = jnp.exp(s - mn)
                ln = a * lo_ + jnp.sum(p, -1, keepdims=True)
                an = a * ao + jnp.dot(p.astype(BF16), v, preferred_element_type=F32)
                out.append((mn, ln, an))
        return tuple(out)

    one = (jnp.full((t, 1), NEG, F32), jnp.zeros((t, 1), F32), jnp.zeros((t, LANES), F32))
    carry = lax.fori_loop(0, jnp.maximum(i - 1, 0),
                          lambda j, cr: tile(pl.multiple_of(j * t, t), lambda h: c_ref[h, 0:1, 0:1], cr),
                          (one,) * (2 * A_HEADS))
    jp = jnp.maximum(i - 1, 0)
    carry = tile(pl.multiple_of(jp * t, t), lambda h: jnp.where(i > 0, wp_ref[h], NEG), carry)
    carry = tile(pl.multiple_of(i * t, t), lambda h: wd_ref[h], carry)

    lv = lam_ref[...]
    lam = (jnp.exp(jnp.sum(lv[0:1] * lv[1:2], -1, keepdims=True))
           - jnp.exp(jnp.sum(lv[2:3] * lv[3:4], -1, keepdims=True)) + lam_init)
    for h in range(A_HEADS):
        c1, c2 = carry[2 * h], carry[2 * h + 1]
        o = c1[2] / c1[1] - lam * (c2[2] / c2[1])
        r = lax.rsqrt(jnp.mean(o * o, -1, keepdims=True) + EPS)
        o_ref[:, h * LANES:(h + 1) * LANES] = ((o * r * g_ref[...]) * (1.0 - lam_init)).astype(o_ref.dtype)


def _diff_attn(qk, vv, lamv, bias, gain, lam_init, bsz, seq):
    t = min(256, seq)
    nq = seq // t
    w = A_HEADS * LANES
    wd, wp, far = bias
    return pl.pallas_call(
        functools.partial(_diff_attn_kernel, t=t, scale=A_DIM ** -0.5, lam_init=lam_init),
        grid=(bsz, nq),
        in_specs=[
            pl.BlockSpec((8, LANES), lambda b, i: (0, 0)),
            pl.BlockSpec((None, t, w), lambda b, i: (b, i, 0)),
            pl.BlockSpec((None, seq, w), lambda b, i: (b, 0, 1)),
            pl.BlockSpec((None, seq, vv.shape[-1]), lambda b, i: (b, 0, 0)),
            pl.BlockSpec((A_HEADS, t, t), lambda b, i: (0, 0, 0)),
            pl.BlockSpec((A_HEADS, t, t), lambda b, i: (0, 0, 0)),
            pl.BlockSpec((A_HEADS, 1, LANES), lambda b, i: (0, 0, 0)),
            pl.BlockSpec((1, LANES), lambda b, i: (0, 0)),
        ],
        out_specs=pl.BlockSpec((None, t, w), lambda b, i: (b, i, 0)),
        out_shape=jax.ShapeDtypeStruct((bsz, seq, w), BF16),
        compiler_params=_cparams("parallel", "arbitrary"),
    )(lamv, qk, qk, vv, wd, wp, far, gain.reshape(1, LANES).astype(F32))


def _gmlp_kernel(u_ref, v_ref, w_ref, b_ref, o_ref, *, chunks):
    row = lax.broadcasted_iota(I32, (B_CHUNK, B_CHUNK), 0)
    col = lax.broadcasted_iota(I32, (B_CHUNK, B_CHUNK), 1)
    ws = [jnp.where(row >= col, w_ref[g], 0.0).astype(BF16) for g in range(B_GROUPS)]
    for c in range(chunks):
        rs = slice(c * B_CHUNK, (c + 1) * B_CHUNK)
        for g in range(B_GROUPS):
            cs = slice(g * B_GROUP_DIM, (g + 1) * B_GROUP_DIM)
            sv = jnp.dot(ws[g], v_ref[rs, cs], preferred_element_type=F32) + b_ref[:, cs]
            o_ref[rs, cs] = (u_ref[rs, cs] * sv).astype(o_ref.dtype)


def _gmlp(u, v, w_s, b_s):
    n, width = u.shape
    chunks = 4
    tm = chunks * B_CHUNK
    bfull = jnp.repeat(b_s.T.astype(F32), B_GROUP_DIM, axis=1)
    return pl.pallas_call(
        functools.partial(_gmlp_kernel, chunks=chunks),
        grid=(n // tm,),
        in_specs=[
            pl.BlockSpec((tm, width), lambda i: (i, 0)),
            pl.BlockSpec((tm, width), lambda i: (i, 0)),
            pl.BlockSpec((B_GROUPS, B_CHUNK, B_CHUNK), lambda i: (0, 0, 0)),
            pl.BlockSpec((B_CHUNK, width), lambda i: (0, 0)),
        ],
        out_specs=pl.BlockSpec((tm, width), lambda i: (i, 0)),
        out_shape=jax.ShapeDtypeStruct((n, width), BF16),
        compiler_params=_cparams("parallel"),
    )(u, v, w_s.astype(F32), bfull)


def _dsa_kernel(cq_ref, ck_ref, cv_ref, iq_ref, ikk_ref, iwq_ref, wd_ref, wp_ref, c_ref, o_ref,
                key_ref, sel_ref, *, t, seq, k_sel, scale, q0):
    i = pl.program_id(1) + q0
    nt = seq // t
    qpos = i * t + lax.broadcasted_iota(I32, (t, seq), 0)
    kpos = lax.broadcasted_iota(I32, (t, seq), 1)
    valid = kpos <= qpos

    ikk = ikk_ref[...].astype(BF16)
    iw = iwq_ref[...]
    score = jnp.zeros((t, seq), F32)
    for pair in range(C_IDX_HEADS // 2):
        qp = iq_ref[:, pair * LANES:(pair + 1) * LANES].astype(BF16)
        lo = lax.broadcasted_iota(I32, qp.shape, 1) < C_IDX_DIM
        zero = jnp.zeros_like(qp)
        for half in range(2):
            h = 2 * pair + half
            qh = jnp.where(lo, qp, zero) if half == 0 else jnp.where(lo, zero, qp)
            d = lax.dot_general(qh, ikk, (((1,), (1,)), ((), ())), preferred_element_type=F32)
            score = score + jnp.maximum(d, 0.0) * iw[:, h:h + 1]

    bits = pltpu.bitcast(score + 0.0, I32)
    skey = bits ^ ((bits >> 31) & 0x7FFFFFFF)
    key_ref[...] = jnp.where(valid, skey, INT_MIN)

    def count(mask):
        return jnp.sum(jnp.where(mask, 1.0, 0.0), axis=-1, keepdims=True)

    def vbody(it, p_u):
        cand = p_u | (jnp.int32(1) << (31 - it))
        cnt = count(key_ref[...] >= (cand ^ INT_MIN))
        return jnp.where(cnt >= k_sel, cand, p_u)

    p_u = lax.fori_loop(0, 32, vbody, jnp.zeros((t, 1), I32))
    thr = p_u ^ INT_MIN
    keys = key_ref[...]
    gt = keys > thr
    eq = (keys == thr) & valid
    need = k_sel - count(gt)
    sel_ref[...] = jnp.where(valid & (keys >= thr), 1.0, 0.0)

    nbits = max(1, (seq - 1).bit_length())

    @pl.when(jnp.max(count(eq) - need) > 0.0)
    def _():
        def ibody(it, m):
            cand = m | (jnp.int32(1) << (nbits - 1 - it))
            cnt = count(eq & (kpos < cand))
            return jnp.where(cnt < need, cand, m)

        m = lax.fori_loop(0, nbits, ibody, jnp.zeros((t, 1), I32))
        sel_ref[...] = jnp.where(valid & (gt | (eq & (kpos <= m))), 1.0, 0.0)

    sel = sel_ref[...] > 0.5

    ck = ck_ref[...]
    cv = cv_ref[...]
    for h in range(C_HEADS):
        far = c_ref[h, 0:1, 0:1]
        tiles = []
        for jt in range(nt):
            tiles.append(jnp.where(i == jt, wd_ref[h], jnp.where(i == jt + 1, wp_ref[h], far)))
        bias = tiles[0] if nt == 1 else jnp.concatenate(tiles, -1)
        s = lax.dot_general(cq_ref[:, h * C_DIM:(h + 1) * C_DIM], ck, (((1,), (1,)), ((), ())),
                            preferred_element_type=F32)
        s = jnp.where(sel, s * scale + bias, NEG)
        p = jnp.exp(s - jnp.max(s, -1, keepdims=True))
        p = p / jnp.sum(p, -1, keepdims=True)
        o_ref[:, h * C_DIM:(h + 1) * C_DIM] = jnp.dot(
            p.astype(BF16), cv, preferred_element_type=F32).astype(o_ref.dtype)


def _dsa(cqk, vv, idx, bias, bsz, seq, k_sel):
    t = LANES
    wd, wp, far = bias
    nq = seq // t
    nrange = math.gcd(nq, DSA_KEY_RANGES)
    per = nq // nrange
    outs = []
    for r in range(nrange):
        q0 = r * per
        sk = (r + 1) * per * t
        qmap = lambda b, i, q0=q0: (b, i + q0, 0)
        outs.append(pl.pallas_call(
            functools.partial(_dsa_kernel, t=t, seq=sk, k_sel=k_sel, scale=C_DIM ** -0.5, q0=q0),
            grid=(bsz, per),
            in_specs=[
                pl.BlockSpec((None, t, C_HEADS * C_DIM), qmap),
                pl.BlockSpec((None, sk, C_DIM), lambda b, i: (b, 0, C_HEADS)),
                pl.BlockSpec((None, sk, C_DIM), lambda b, i: (b, 0, C_HEADS)),
                pl.BlockSpec((None, t, C_IDX_HEADS * C_IDX_DIM), qmap),
                pl.BlockSpec((None, sk, LANES), lambda b, i: (b, 0, 4)),
                pl.BlockSpec((None, t, LANES), lambda b, i, q0=q0: (b, i + q0, 5)),
                pl.BlockSpec((C_HEADS, t, t), lambda b, i: (0, 0, 0)),
                pl.BlockSpec((C_HEADS, t, t), lambda b, i: (0, 0, 0)),
                pl.BlockSpec((C_HEADS, 1, LANES), lambda b, i: (0, 0, 0)),
            ],
            out_specs=pl.BlockSpec((None, t, C_HEADS * C_DIM), lambda b, i: (b, i, 0)),
            out_shape=jax.ShapeDtypeStruct((bsz, per * t, C_HEADS * C_DIM), BF16),
            scratch_shapes=[pltpu.VMEM((t, sk), I32), pltpu.VMEM((t, sk), F32)],
            compiler_params=_cparams("parallel", "arbitrary"),
        )(cqk, cqk, vv, idx, idx, idx, wd, wp, far))
    return outs[0] if nrange == 1 else jnp.concatenate(outs, axis=1)


def _merge_kernel(x_ref, ya_ref, yb_ref, yc_ref, g_ref, wa_ref, wb_ref, wc_ref, wo_ref, o_ref, *, d):
    merged = (g_ref[:, 0:d] * jnp.dot(ya_ref[...], wa_ref[...], preferred_element_type=F32)
              + g_ref[:, d:2 * d] * jnp.dot(yb_ref[...], wb_ref[...], preferred_element_type=F32)
              + g_ref[:, 2 * d:3 * d] * jnp.dot(yc_ref[...], wc_ref[...], preferred_element_type=F32))
    o_ref[...] = x_ref[...] + jnp.dot(merged.astype(BF16), wo_ref[...], preferred_element_type=F32)


def _merge(x, ya, yb, yc, gates, wa, wb, wc, wo):
    n, d = x.shape
    bw = ya.shape[1]
    tm = min(512, n)
    row = lambda w: pl.BlockSpec((tm, w), lambda i: (i, 0))
    full = lambda a, b: pl.BlockSpec((a, b), lambda i: (0, 0))
    return pl.pallas_call(
        functools.partial(_merge_kernel, d=d),
        grid=(n // tm,),
        in_specs=[row(d), row(bw), row(bw), row(bw), row(3 * d),
                  full(bw, d), full(bw, d), full(bw, d), full(d, d)],
        out_specs=row(d),
        out_shape=jax.ShapeDtypeStruct((n, d), F32),
        compiler_params=_cparams("parallel"),
    )(x, ya, yb, yc, gates, wa.astype(BF16), wb.astype(BF16), wc.astype(BF16), wo.astype(BF16))


def _mem_attn_kernel(x_ref, q_ref, k_ref, v_ref, wo_ref, o_ref, *, scale):
    outs = []
    for h in range(M_HEADS):
        cs = slice(h * M_DIM, (h + 1) * M_DIM)
        s = lax.dot_general(q_ref[:, cs], k_ref[:, cs], (((1,), (1,)), ((), ())),
                            preferred_element_type=F32) * scale
        p = jnp.exp(s - jnp.max(s, -1, keepdims=True))
        p = p / jnp.sum(p, -1, keepdims=True)
        outs.append(jnp.dot(p.astype(BF16), v_ref[:, cs], preferred_element_type=F32).astype(BF16))
    o = jnp.concatenate(outs, -1)
    o_ref[...] = x_ref[...] + jnp.dot(o, wo_ref[...], preferred_element_type=F32)


def _mem_attn(x, q, k, v, wo, bsz, seq, mlen):
    d = x.shape[-1]
    w = M_HEADS * M_DIM
    t = min(512, seq)
    return pl.pallas_call(
        functools.partial(_mem_attn_kernel, scale=M_DIM ** -0.5),
        grid=(bsz, seq // t),
        in_specs=[
            pl.BlockSpec((None, t, d), lambda b, i: (b, i, 0)),
            pl.BlockSpec((None, t, w), lambda b, i: (b, i, 0)),
            pl.BlockSpec((None, mlen, w), lambda b, i: (b, 0, 0)),
            pl.BlockSpec((None, mlen, w), lambda b, i: (b, 0, 0)),
            pl.BlockSpec((w, d), lambda b, i: (0, 0)),
        ],
        out_specs=pl.BlockSpec((None, t, d), lambda b, i: (b, i, 0)),
        out_shape=jax.ShapeDtypeStruct((bsz, seq, d), F32),
        compiler_params=_cparams("parallel", "arbitrary"),
    )(x, q, k, v, wo.astype(BF16))


def _topk_rows_ids(arr, ids, k):
    vals, idxs = [], []
    for _ in range(k):
        m = jnp.max(arr, axis=0, keepdims=True)
        am = jnp.min(jnp.where(arr == m, ids, 2 ** 30), axis=0, keepdims=True)
        vals.append(m)
        idxs.append(am)
        arr = jnp.where(ids == am, -jnp.inf, arr)
    return jnp.concatenate(vals, 0), jnp.concatenate(idxs, 0)


def _topk_rows(arr, k):
    return _topk_rows_ids(arr, lax.broadcasted_iota(I32, arr.shape, 0), k)


def _topk_pair_sums(v1, v2):
    k = P_TOPK
    t = v1.shape[1]
    r8 = lax.broadcasted_iota(I32, (8, t), 0)
    r16 = lax.broadcasted_iota(I32, (16, t), 0)
    vals = [v1[0:1] + v2]
    ids = [r16]
    for a in (1, 2, 3):
        vals.append(v1[a:a + 1] + v2[0:8])
        ids.append(a * k + r8)
    for b in (0, 1, 2):
        vals.append(jnp.where(r8 >= 4, v1[0:8] + v2[b:b + 1], -jnp.inf))
        ids.append(r8 * k + b)
    vals.append(v1[8:16] + v2[0:1])
    ids.append((r8 + 8) * k)
    return _topk_rows_ids(jnp.concatenate(vals, 0), jnp.concatenate(ids, 0), k)


def _pick_rows(table, sel):
    out = jnp.zeros(sel.shape, table.dtype)
    for r in range(table.shape[0]):
        out = jnp.where(sel == r, table[r:r + 1, :], out)
    return out


def _peer_topk_kernel(q_ref, k1_ref, k2_ref, ids_ref, gate_ref, *, heads):
    half = P_QDIM // 2
    dn = (((1,), (1,)), ((), ()))
    for h in range(heads):
        q1 = q_ref[:, h * P_QDIM:h * P_QDIM + half].astype(BF16)
        q2 = q_ref[:, h * P_QDIM + half:(h + 1) * P_QDIM].astype(BF16)
        s1 = lax.dot_general(k1_ref[...], q1, dn, preferred_element_type=F32)
        s2 = lax.dot_general(k2_ref[...], q2, dn, preferred_element_type=F32)
        v1, i1 = _topk_rows(s1, P_TOPK)
        v2, i2 = _topk_rows(s2, P_TOPK)
        cs, ci = _topk_pair_sums(v1, v2)
        e1 = _pick_rows(i1, ci >> 4)
        e2 = _pick_rows(i2, ci & (P_TOPK - 1))
        rows = slice(h * P_TOPK, (h + 1) * P_TOPK)
        ids_ref[rows, :] = e1 * P_NKEYS + e2
        p = jnp.exp(cs - jnp.max(cs, axis=0, keepdims=True))
        gate_ref[rows, :] = p / jnp.sum(p, axis=0, keepdims=True)


def _peer_topk(q, sk1, sk2):
    n = q.shape[0]
    t = LANES
    assert P_TOPK == 16
    hb = PEER_TOPK_HEADS_PER_STEP
    return pl.pallas_call(
        functools.partial(_peer_topk_kernel, heads=hb),
        grid=(n // t, P_HEADS // hb),
        in_specs=[
            pl.BlockSpec((t, hb * P_QDIM), lambda i, h: (i, h)),
            pl.BlockSpec((P_NKEYS, P_QDIM // 2), lambda i, h: (0, 0)),
            pl.BlockSpec((P_NKEYS, P_QDIM // 2), lambda i, h: (0, 0)),
        ],
        out_specs=[
            pl.BlockSpec((hb * P_TOPK, t), lambda i, h: (h, i)),
            pl.BlockSpec((hb * P_TOPK, t), lambda i, h: (h, i)),
        ],
        out_shape=[
            jax.ShapeDtypeStruct((P_HEADS * P_TOPK, n), I32),
            jax.ShapeDtypeStruct((P_HEADS * P_TOPK, n), F32),
        ],
        compiler_params=_cparams("parallel", "arbitrary"),
    )(q, sk1.astype(BF16), sk2.astype(BF16))


PEER_TOKENS_PER_STEP = 128
PEER_TOKENS_PER_SLOT = 8


def _pack_tables(u_tab, v_tab):
    ub = lax.bitcast_convert_type(u_tab.astype(BF16), jnp.uint16).astype(jnp.uint32)
    vb = lax.bitcast_convert_type(v_tab.astype(BF16), jnp.uint16).astype(jnp.uint32)
    return (ub | (vb << 16)).reshape(-1, LANES)


def _peer_expert_kernel(ids_hbm, x_ref, g_ref, gate_ref, gsum_ref, gexp_ref, uv_hbm, o_ref,
                        ids_smem, xn_ref, buf0, buf1, sem_ids, sem_row, *, tt, tb, npair, nsteps):
    i = pl.program_id(0)
    rows = tb * npair
    nsub = tt // tb
    m = tt * npair
    sub = x_ref.shape[1]
    bufs = (buf0, buf1)
    cur = i & 1
    has_next = i + 1 < nsteps

    def ids_copy(step, half):
        return pltpu.make_async_copy(ids_hbm.at[step], ids_smem.at[pl.ds(half * m, m)], sem_ids.at[half])

    def issue_token(ids_base, j, dst, slot):
        tok_base = ids_base + j * npair
        for r in range(npair):
            e = ids_smem[tok_base + r]
            pltpu.make_async_copy(uv_hbm.at[pl.ds(pl.multiple_of(e * sub, sub), sub)],
                                  dst.at[pl.ds(pl.multiple_of((j * npair + r) * sub, sub), sub)],
                                  sem_row.at[slot]).start(priority=r % 2)

    def wait_rows(slot):
        pltpu.make_async_copy(bufs[1 - slot], bufs[slot], sem_row.at[slot]).wait()

    @pl.when(i == 0)
    def _():
        first = ids_copy(0, 0)
        first.start()
        first.wait()

        def body(j, carry):
            issue_token(0, j, buf0, 0)
            return carry
        lax.fori_loop(0, tb, body, 0)

    @pl.when(has_next)
    def _():
        ids_copy(i + 1, 1 - cur).start()

    x = x_ref[...]
    ss = jnp.sum(jnp.sum(x * x, axis=2, keepdims=True), axis=1, keepdims=True)
    xn_ref[...] = x * lax.rsqrt(ss * (1.0 / (sub * LANES)) + EPS) * g_ref[...]
    ncol = npair * sub
    diag = (lax.broadcasted_iota(I32, (sub, ncol), 1) & (sub - 1)) == lax.broadcasted_iota(I32, (sub, ncol), 0)

    def sub_batch(sb, slot):
        src, dst = bufs[slot], bufs[1 - slot]
        wait_rows(slot)
        is_last = sb == nsub - 1

        @pl.when(jnp.logical_and(is_last, has_next))
        def _():
            ids_copy(i + 1, 1 - cur).wait()

        nxt_half = jnp.where(jnp.logical_and(is_last, has_next), 1 - cur, cur)
        nxt_sb = jnp.where(is_last, jnp.where(has_next, 0, sb), sb + 1)
        ids_base = nxt_half * m + nxt_sb * rows

        t0 = pl.multiple_of(sb * tb, tb)
        parts = []
        for j in range(tb):
            issue_token(ids_base, j, dst, 1 - slot)
            w = src[j * ncol:(j + 1) * ncol, :]
            u = pltpu.bitcast(w << 16, F32).astype(BF16)
            xt = xn_ref[t0 + j].astype(BF16)
            pt = lax.dot_general(xt, u, (((1,), (1,)), ((), ())), preferred_element_type=F32)
            parts.append(jnp.where(diag, pt, 0.0))
        pm = jnp.concatenate(parts, 0)
        hi = pm.astype(BF16)
        lo = (pm - hi.astype(F32)).astype(BF16)
        h = (jnp.dot(hi, gsum_ref[...], preferred_element_type=F32)
             + jnp.dot(lo, gsum_ref[...], preferred_element_type=F32))
        h = jnp.sum(h.reshape(tb, sub, npair), axis=1)
        coef = (gate_ref[pl.ds(t0, tb), :] * _gelu(h)).astype(BF16)
        cexp = jnp.dot(coef, gexp_ref[...], preferred_element_type=F32)
        for j in range(tb):
            cs = jnp.where(diag, jnp.broadcast_to(cexp[j:j + 1], (sub, ncol)), 0.0).astype(BF16)
            w = src[j * ncol:(j + 1) * ncol, :]
            v = pltpu.bitcast(w & jnp.uint32(0xFFFF0000), F32).astype(BF16)
            o_ref[t0 + j] = x_ref[t0 + j] + jnp.dot(cs, v, preferred_element_type=F32)

    def pair(sp, carry):
        sub_batch(2 * sp, 0)
        sub_batch(2 * sp + 1, 1)
        return carry

    lax.fori_loop(0, nsub // 2, pair, 0)

    @pl.when(i == nsteps - 1)
    def _():
        wait_rows(0)


def _peer_expert(x, g, ids_t, gates_t, uv_tab):
    n, d = x.shape
    npair = ids_t.shape[0]
    sub = d // LANES
    tt, tb = PEER_TOKENS_PER_STEP, PEER_TOKENS_PER_SLOT
    assert n % tt == 0 and tt % (2 * tb) == 0 and sub == 8
    nsteps = n // tt
    ids = ids_t.T.reshape(nsteps, tt * npair)
    ncol = npair * sub
    gsum = (jnp.arange(ncol, dtype=I32)[:, None] // sub == jnp.arange(npair, dtype=I32)[None, :]).astype(BF16)
    tile = lambda: pl.BlockSpec((tt, sub, LANES), lambda i: (i, 0, 0))
    out = pl.pallas_call(
        functools.partial(_peer_expert_kernel, tt=tt, tb=tb, npair=npair, nsteps=nsteps),
        grid=(nsteps,),
        in_specs=[
            pl.BlockSpec(memory_space=pl.ANY),
            tile(),
            pl.BlockSpec((sub, LANES), lambda i: (0, 0)),
            pl.BlockSpec((tt, npair), lambda i: (i, 0)),
            pl.BlockSpec((ncol, npair), lambda i: (0, 0)),
            pl.BlockSpec((npair, ncol), lambda i: (0, 0)),
            pl.BlockSpec(memory_space=pl.ANY),
        ],
        out_specs=tile(),
        out_shape=jax.ShapeDtypeStruct((n, sub, LANES), F32),
        scratch_shapes=[
            pltpu.SMEM((2 * tt * npair,), I32),
            pltpu.VMEM((tt, sub, LANES), F32),
            pltpu.VMEM((tb * ncol, LANES), jnp.uint32),
            pltpu.VMEM((tb * ncol, LANES), jnp.uint32),
            pltpu.SemaphoreType.DMA((2,)),
            pltpu.SemaphoreType.DMA((2,)),
        ],
        compiler_params=_cparams("arbitrary"),
    )(ids, x.reshape(n, sub, LANES), g.reshape(sub, LANES).astype(F32), gates_t.T, gsum, gsum.T, uv_tab)
    return out.reshape(n, d)


def _tile_gain(g, reps):
    return jnp.tile(g.astype(F32), reps)


def _layer(l, x, memn_kv, bias_a, bias_c, p):
    bsz, seq, d = x.shape
    n = bsz * seq
    lam_init = 0.8 - 0.6 * math.exp(-0.3 * l)
    x2 = x.reshape(n, d)
    w_in = p["w_in"][l]
    cols = np.cumsum([0, 512, 512, 512, 512, 512, 512, 128, 128, 512, 64, 8, 3 * d])
    seg = lambda a, b: w_in[:, cols[a]:cols[b]]
    gmix = p["norm_mix"][l]

    qk = _proj(x2, gmix, seg(0, 2), "norm64",
               jnp.concatenate([_tile_gain(p["a_q_gain"][l], 8), _tile_gain(p["a_k_gain"][l], 8)]), BF16)
    vv = _proj(x2, gmix, jnp.concatenate([seg(2, 3), seg(7, 8)], 1), "none", None, BF16, tn=640)
    u = _proj(x2, gmix, seg(3, 4), "gelu", None, F32)
    v = _proj(x2, gmix, seg(4, 5), "gelu_norm128", p["b_v_gain"][l], BF16)
    cqk = _proj(x2, gmix, seg(5, 7), "norm128",
                jnp.concatenate([_tile_gain(p["c_q_gain"][l], 4), p["c_k_gain"][l].astype(F32)]), BF16, tn=640)
    w_idx = jnp.concatenate([seg(8, 9), seg(9, 10), seg(9, 10), seg(10, 11),
                             jnp.zeros((d, 120), w_in.dtype)], 1)
    idx = _proj(x2, gmix, w_idx, "none", None, F32, tn=768)
    gates = _proj(x2, gmix, seg(11, 12), "sigmoid", None, F32)

    lamv = jnp.zeros((8, LANES), F32)
    for r, name in enumerate(("a_lq1", "a_lk1", "a_lq2", "a_lk2")):
        lamv = lamv.at[r, :A_DIM].set(p[name][l].astype(F32))
    ya = _diff_attn(qk.reshape(bsz, seq, -1), vv.reshape(bsz, seq, -1), lamv, bias_a,
                    p["a_subln_gain"][l], lam_init, bsz, seq)
    yb = _gmlp(u, v, p["b_w_s"][l], p["b_b_s"][l])
    k_sel = min(C_TOPK_MAX, seq // 4)
    yc = _dsa(cqk.reshape(bsz, seq, -1), vv.reshape(bsz, seq, -1), idx.reshape(bsz, seq, -1),
              bias_c, bsz, seq, k_sel)
    x2 = _merge(x2, ya.reshape(n, -1), yb, yc.reshape(n, -1), gates,
                p["w_br_a"][l], p["w_br_b"][l], p["w_br_c"][l], p["w_mix_out"][l])

    mk, mv = memn_kv
    mq = _proj(x2, p["norm_mem"][l], p["m_wq"][l], "norm128", _tile_gain(p["m_q_gain"][l], M_HEADS), BF16)
    x3 = _mem_attn(x2.reshape(bsz, seq, d), mq.reshape(bsz, seq, -1), mk, mv, p["m_wo"][l],
                   bsz, seq, mk.shape[1])
    x2 = x3.reshape(n, d)

    pq = _proj(x2, p["norm_peer"][l], p["p_wq"][l], "none", None, F32)
    ids_t, gates_t = _peer_topk(pq, p["p_subkey1"][l], p["p_subkey2"][l])
    x2 = _peer_expert(x2, p["norm_peer"][l], ids_t, gates_t, _pack_tables(p["p_u"][l], p["p_v"][l]))
    return x2.reshape(bsz, seq, d)


def kernel(x, mem, rel_bias, norm_mix, w_in, a_q_gain, a_k_gain, a_lq1, a_lk1, a_lq2, a_lk2, a_subln_gain, b_v_gain, b_w_s, b_b_s, c_q_gain, c_k_gain, w_br_a, w_br_b, w_br_c, w_mix_out, norm_mem, norm_memsrc, m_wq, m_wkv, m_q_gain, m_k_gain, m_wo, norm_peer, p_wq, p_subkey1, p_subkey2, p_u, p_v):
    p = dict(norm_mix=norm_mix, w_in=w_in, a_q_gain=a_q_gain, a_k_gain=a_k_gain, a_lq1=a_lq1,
             a_lk1=a_lk1, a_lq2=a_lq2, a_lk2=a_lk2, a_subln_gain=a_subln_gain, b_v_gain=b_v_gain,
             b_w_s=b_w_s, b_b_s=b_b_s, c_q_gain=c_q_gain, c_k_gain=c_k_gain, w_br_a=w_br_a,
             w_br_b=w_br_b, w_br_c=w_br_c, w_mix_out=w_mix_out, norm_mem=norm_mem, m_wq=m_wq,
             m_q_gain=m_q_gain, m_wo=m_wo, norm_peer=norm_peer, p_wq=p_wq, p_subkey1=p_subkey1,
             p_subkey2=p_subkey2, p_u=p_u, p_v=p_v)
    bsz, seq, d = x.shape
    mlen = mem.shape[1]
    depth = w_in.shape[0]
    bias_a = _bias_tiles(rel_bias, slice(0, A_HEADS), min(256, seq), seq)
    bias_c = _bias_tiles(rel_bias, slice(A_HEADS, A_HEADS + C_HEADS), LANES, seq)
    mem2 = mem.reshape(bsz * mlen, d)
    w = M_HEADS * M_DIM
    for l in range(depth):
        mk = _proj(mem2, norm_memsrc[l], m_wkv[l][:, :w], "norm128", _tile_gain(m_k_gain[l], M_HEADS), BF16)
        mv = _proj(mem2, norm_memsrc[l], m_wkv[l][:, w:], "none", None, BF16)
        x = _layer(l, x, (mk.reshape(bsz, mlen, w), mv.reshape(bsz, mlen, w)), bias_a, bias_c, p)
    return x
```

```python
import functools
import math

import jax
import jax.numpy as jnp
import numpy as np
from jax import lax
from jax.experimental import pallas as pl
from jax.experimental.pallas import tpu as pltpu

F32 = jnp.float32
BF16 = jnp.bfloat16
I32 = jnp.int32

EPS = 1e-6
NEG = -1e30
INT_MIN = -(2 ** 31)

A_HEADS = 4
A_DIM = 64
B_GROUPS = 4
B_GROUP_DIM = 128
B_CHUNK = 128
C_HEADS = 4
C_DIM = 128
C_IDX_HEADS = 8
C_IDX_DIM = 64
C_TOPK_MAX = 256
M_HEADS = 4
M_DIM = 128
P_HEADS = 8
P_QDIM = 256
P_NKEYS = 128
P_TOPK = 16
REL_BUCKETS = 32
REL_MAX_DIST = 128
LANES = 128

DSA_KEY_RANGES = 8
PEER_TOPK_HEADS_PER_STEP = 4

VMEM_LIMIT = 56 * 1024 * 1024


def _cparams(*sem):
    return pltpu.CompilerParams(dimension_semantics=sem, vmem_limit_bytes=VMEM_LIMIT)


def _gelu(x):
    return 0.5 * x * (1.0 + lax.erf(x * (1.0 / math.sqrt(2.0))))


def _group_norm(h, gain, group):
    parts = []
    for c in range(h.shape[-1] // LANES):
        hc = h[:, c * LANES:(c + 1) * LANES]
        sq = hc * hc
        if group == LANES:
            r = lax.rsqrt(jnp.sum(sq, -1, keepdims=True) * (1.0 / LANES) + EPS)
        else:
            lo = lax.broadcasted_iota(I32, hc.shape, 1) < 64
            s_lo = jnp.sum(jnp.where(lo, sq, 0.0), -1, keepdims=True)
            s_hi = jnp.sum(jnp.where(lo, 0.0, sq), -1, keepdims=True)
            r = jnp.where(lo, lax.rsqrt(s_lo * (1.0 / 64) + EPS), lax.rsqrt(s_hi * (1.0 / 64) + EPS))
        parts.append(hc * r)
    out = parts[0] if len(parts) == 1 else jnp.concatenate(parts, -1)
    return out * gain


def _epilogue(h, gain, mode):
    if mode == "none":
        return h
    if mode == "norm64":
        return _group_norm(h, gain, 64)
    if mode == "norm128":
        return _group_norm(h, gain, 128)
    if mode == "gelu":
        return _gelu(h)
    if mode == "gelu_norm128":
        return _group_norm(_gelu(h), gain, 128)
    if mode == "sigmoid":
        return jax.nn.sigmoid(h)
    raise ValueError(mode)


def _proj_kernel(x_ref, g_ref, w_ref, e_ref, o_ref, xn_ref, *, mode):
    @pl.when(pl.program_id(1) == 0)
    def _():
        x = x_ref[...]
        ms = jnp.mean(x * x, axis=-1, keepdims=True)
        xn_ref[...] = (x * lax.rsqrt(ms + EPS) * g_ref[...]).astype(BF16)

    h = jnp.dot(xn_ref[...], w_ref[...], preferred_element_type=F32)
    o_ref[...] = _epilogue(h, e_ref[...], mode).astype(o_ref.dtype)


def _proj(x, g, w, mode="none", gain=None, out_dtype=F32, tn=512):
    n, d = x.shape
    dout = w.shape[1]
    tm = min(1024, n)
    assert n % tm == 0 and dout % tn == 0, (n, dout, tn)
    if gain is None:
        gain = jnp.ones((dout,), F32)
    return pl.pallas_call(
        functools.partial(_proj_kernel, mode=mode),
        grid=(n // tm, dout // tn),
        in_specs=[
            pl.BlockSpec((tm, d), lambda i, j: (i, 0)),
            pl.BlockSpec((1, d), lambda i, j: (0, 0)),
            pl.BlockSpec((d, tn), lambda i, j: (0, j)),
            pl.BlockSpec((1, tn), lambda i, j: (0, j)),
        ],
        out_specs=pl.BlockSpec((tm, tn), lambda i, j: (i, j)),
        out_shape=jax.ShapeDtypeStruct((n, dout), out_dtype),
        scratch_shapes=[pltpu.VMEM((tm, d), BF16)],
        compiler_params=_cparams("parallel", "arbitrary"),
    )(x, g.reshape(1, d).astype(F32), w.astype(BF16), gain.reshape(1, dout).astype(F32))


def _t5_bucket(dist):
    n = jnp.maximum(dist, 0)
    max_exact = REL_BUCKETS // 2
    nf = jnp.maximum(n, 1).astype(F32)
    large = max_exact + (jnp.log(nf / max_exact) / math.log(REL_MAX_DIST / max_exact)
                         * (REL_BUCKETS - max_exact)).astype(I32)
    large = jnp.minimum(large, REL_BUCKETS - 1)
    return jnp.where(n < max_exact, n, large)


def _far_bucket_is_constant(t, seq):
    d = np.arange(t + 1, max(seq, t + 2), dtype=np.float64)
    max_exact = REL_BUCKETS // 2
    large = max_exact + np.floor(np.log(d / max_exact) / math.log(REL_MAX_DIST / max_exact)
                                 * (REL_BUCKETS - max_exact) + 1e-6).astype(np.int64)
    safe = max_exact + np.floor(np.log(d / max_exact) / math.log(REL_MAX_DIST / max_exact)
                                * (REL_BUCKETS - max_exact) - 1e-3).astype(np.int64)
    return bool(np.all(np.minimum(large, REL_BUCKETS - 1) == REL_BUCKETS - 1)
                and np.all(np.minimum(safe, REL_BUCKETS - 1) == REL_BUCKETS - 1))


def _bias_tiles(rel_bias, heads, t, seq):
    assert _far_bucket_is_constant(t, seq)
    tab = rel_bias[_t5_bucket(jnp.arange(2 * t, dtype=I32))][:, heads].astype(F32).T
    nh = tab.shape[0]
    period = 2 * t

    def toeplitz(w):
        rep = jnp.tile(w, (1, t))[:, :t * (period - 1)]
        return rep.reshape(nh, t, period - 1)[:, :, :t]

    back = tab[:, 1:t][:, ::-1]
    w_diag = jnp.concatenate([tab[:, 0:1], jnp.full((nh, t), NEG, F32), back], axis=1)
    w_prev = jnp.concatenate([tab[:, 1:t + 1][:, ::-1], tab[:, t:t + 1], tab[:, t + 1:][:, ::-1]], axis=1)
    far = rel_bias[REL_BUCKETS - 1, heads].astype(F32)
    return toeplitz(w_diag), toeplitz(w_prev), jnp.broadcast_to(far[:, None, None], (nh, 1, LANES))


def _diff_attn_kernel(lam_ref, q_ref, k_ref, v_ref, wd_ref, wp_ref, c_ref, g_ref, o_ref, *,
                      t, scale, lam_init):
    i = pl.program_id(1)
    lo = lax.broadcasted_iota(I32, (t, LANES), 1) < A_DIM
    qs = []
    for h in range(A_HEADS):
        q = q_ref[:, h * LANES:(h + 1) * LANES]
        zero = jnp.zeros_like(q)
        qs.append((jnp.where(lo, q, zero), jnp.where(lo, zero, q)))

    def tile(start, bias_of, carry):
        out = []
        for h in range(A_HEADS):
            k = k_ref[pl.ds(start, t), h * LANES:(h + 1) * LANES]
            v = v_ref[pl.ds(start, t), h * LANES:(h + 1) * LANES]
            bias = bias_of(h)
            for m in range(2):
                mo, lo_, ao = carry[2 * h + m]
                s = lax.dot_general(qs[h][m], k, (((1,), (1,)), ((), ())), preferred_element_type=F32)
                s = s * scale + bias
                mn = jnp.maximum(mo, jnp.max(s, -1, keepdims=True))
                a = jnp.exp(mo - mn)
                p = jnp.exp(s - mn)
                ln = a * lo_ + jnp.sum(p, -1, keepdims=True)
                an = a * ao + jnp.dot(p.astype(BF16), v, preferred_element_type=F32)
                out.append((mn, ln, an))
        return tuple(out)

    one = (jnp.full((t, 1), NEG, F32), jnp.zeros((t, 1), F32), jnp.zeros((t, LANES), F32))
    carry = lax.fori_loop(0, jnp.maximum(i - 1, 0),
                          lambda j, cr: tile(pl.multiple_of(j * t, t), lambda h: c_ref[h, 0:1, 0:1], cr),
                          (one,) * (2 * A_HEADS))
    jp = jnp.maximum(i - 1, 0)
    carry = tile(pl.multiple_of(jp * t, t), lambda h: jnp.where(i > 0, wp_ref[h], NEG), carry)
    carry = tile(pl.multiple_of(i * t, t), lambda h: wd_ref[h], carry)

    lv = lam_ref[...]
    lam = (jnp.exp(jnp.sum(lv[0:1] * lv[1:2], -1, keepdims=True))
           - jnp.exp(jnp.sum(lv[2:3] * lv[3:4], -1, keepdims=True)) + lam_init)
    for h in range(A_HEADS):
        c1, c2 = carry[2 * h], carry[2 * h + 1]
        o = c1[2] / c1[1] - lam * (c2[2] / c2[1])
        r = lax.rsqrt(jnp.mean(o * o, -1, keepdims=True) + EPS)
        o_ref[:, h * LANES:(h + 1) * LANES] = ((o * r * g_ref[...]) * (1.0 - lam_init)).astype(o_ref.dtype)


def _diff_attn(qk, vv, lamv, bias, gain, lam_init, bsz, seq):
    t = min(256, seq)
    nq = seq // t
    w = A_HEADS * LANES
    wd, wp, far = bias
    return pl.pallas_call(
        functools.partial(_diff_attn_kernel, t=t, scale=A_DIM ** -0.5, lam_init=lam_init),
        grid=(bsz, nq),
        in_specs=[
            pl.BlockSpec((8, LANES), lambda b, i: (0, 0)),
            pl.BlockSpec((None, t, w), lambda b, i: (b, i, 0)),
            pl.BlockSpec((None, seq, w), lambda b, i: (b, 0, 1)),
            pl.BlockSpec((None, seq, vv.shape[-1]), lambda b, i: (b, 0, 0)),
            pl.BlockSpec((A_HEADS, t, t), lambda b, i: (0, 0, 0)),
            pl.BlockSpec((A_HEADS, t, t), lambda b, i: (0, 0, 0)),
            pl.BlockSpec((A_HEADS, 1, LANES), lambda b, i: (0, 0, 0)),
            pl.BlockSpec((1, LANES), lambda b, i: (0, 0)),
        ],
        out_specs=pl.BlockSpec((None, t, w), lambda b, i: (b, i, 0)),
        out_shape=jax.ShapeDtypeStruct((bsz, seq, w), BF16),
        compiler_params=_cparams("parallel", "arbitrary"),
    )(lamv, qk, qk, vv, wd, wp, far, gain.reshape(1, LANES).astype(F32))


def _gmlp_kernel(u_ref, v_ref, w_ref, b_ref, o_ref, *, chunks):
    row = lax.broadcasted_iota(I32, (B_CHUNK, B_CHUNK), 0)
    col = lax.broadcasted_iota(I32, (B_CHUNK, B_CHUNK), 1)
    ws = [jnp.where(row >= col, w_ref[g], 0.0).astype(BF16) for g in range(B_GROUPS)]
    for c in range(chunks):
        rs = slice(c * B_CHUNK, (c + 1) * B_CHUNK)
        for g in range(B_GROUPS):
            cs = slice(g * B_GROUP_DIM, (g + 1) * B_GROUP_DIM)
            sv = jnp.dot(ws[g], v_ref[rs, cs], preferred_element_type=F32) + b_ref[:, cs]
            o_ref[rs, cs] = (u_ref[rs, cs] * sv).astype(o_ref.dtype)


def _gmlp(u, v, w_s, b_s):
    n, width = u.shape
    chunks = 4
    tm = chunks * B_CHUNK
    bfull = jnp.repeat(b_s.T.astype(F32), B_GROUP_DIM, axis=1)
    return pl.pallas_call(
        functools.partial(_gmlp_kernel, chunks=chunks),
        grid=(n // tm,),
        in_specs=[
            pl.BlockSpec((tm, width), lambda i: (i, 0)),
            pl.BlockSpec((tm, width), lambda i: (i, 0)),
            pl.BlockSpec((B_GROUPS, B_CHUNK, B_CHUNK), lambda i: (0, 0, 0)),
            pl.BlockSpec((B_CHUNK, width), lambda i: (0, 0)),
        ],
        out_specs=pl.BlockSpec((tm, width), lambda i: (i, 0)),
        out_shape=jax.ShapeDtypeStruct((n, width), BF16),
        compiler_params=_cparams("parallel"),
    )(u, v, w_s.astype(F32), bfull)


def _dsa_kernel(cq_ref, ck_ref, cv_ref, iq_ref, ikk_ref, iwq_ref, wd_ref, wp_ref, c_ref, o_ref,
                key_ref, sel_ref, *, t, seq, k_sel, scale, q0):
    i = pl.program_id(1) + q0
    nt = seq // t
    qpos = i * t + lax.broadcasted_iota(I32, (t, seq), 0)
    kpos = lax.broadcasted_iota(I32, (t, seq), 1)
    valid = kpos <= qpos

    ikk = ikk_ref[...].astype(BF16)
    iw = iwq_ref[...]
    score = jnp.zeros((t, seq), F32)
    for pair in range(C_IDX_HEADS // 2):
        qp = iq_ref[:, pair * LANES:(pair + 1) * LANES].astype(BF16)
        lo = lax.broadcasted_iota(I32, qp.shape, 1) < C_IDX_DIM
        zero = jnp.zeros_like(qp)
        for half in range(2):
            h = 2 * pair + half
            qh = jnp.where(lo, qp, zero) if half == 0 else jnp.where(lo, zero, qp)
            d = lax.dot_general(qh, ikk, (((1,), (1,)), ((), ())), preferred_element_type=F32)
            score = score + jnp.maximum(d, 0.0) * iw[:, h:h + 1]

    bits = pltpu.bitcast(score + 0.0, I32)
    skey = bits ^ ((bits >> 31) & 0x7FFFFFFF)
    key_ref[...] = jnp.where(valid, skey, INT_MIN)

    def count(mask):
        return jnp.sum(jnp.where(mask, 1.0, 0.0), axis=-1, keepdims=True)

    def vbody(it, p_u):
        cand = p_u | (jnp.int32(1) << (31 - it))
        cnt = count(key_ref[...] >= (cand ^ INT_MIN))
        return jnp.where(cnt >= k_sel, cand, p_u)

    p_u = lax.fori_loop(0, 32, vbody, jnp.zeros((t, 1), I32))
    thr = p_u ^ INT_MIN
    keys = key_ref[...]
    gt = keys > thr
    eq = (keys == thr) & valid
    need = k_sel - count(gt)
    sel_ref[...] = jnp.where(valid & (keys >= thr), 1.0, 0.0)

    nbits = max(1, (seq - 1).bit_length())

    @pl.when(jnp.max(count(eq) - need) > 0.0)
    def _():
        def ibody(it, m):
            cand = m | (jnp.int32(1) << (nbits - 1 - it))
            cnt = count(eq & (kpos < cand))
            return jnp.where(cnt < need, cand, m)

        m = lax.fori_loop(0, nbits, ibody, jnp.zeros((t, 1), I32))
        sel_ref[...] = jnp.where(valid & (gt | (eq & (kpos <= m))), 1.0, 0.0)

    sel = sel_ref[...] > 0.5

    ck = ck_ref[...]
    cv = cv_ref[...]
    for h in range(C_HEADS):
        far = c_ref[h, 0:1, 0:1]
        tiles = []
        for jt in range(nt):
            tiles.append(jnp.where(i == jt, wd_ref[h], jnp.where(i == jt + 1, wp_ref[h], far)))
        bias = tiles[0] if nt == 1 else jnp.concatenate(tiles, -1)
        s = lax.dot_general(cq_ref[:, h * C_DIM:(h + 1) * C_DIM], ck, (((1,), (1,)), ((), ())),
                            preferred_element_type=F32)
        s = jnp.where(sel, s * scale + bias, NEG)
        p = jnp.exp(s - jnp.max(s, -1, keepdims=True))
        p = p / jnp.sum(p, -1, keepdims=True)
        o_ref[:, h * C_DIM:(h + 1) * C_DIM] = jnp.dot(
            p.astype(BF16), cv, preferred_element_type=F32).astype(o_ref.dtype)


def _dsa(cqk, vv, idx, bias, bsz, seq, k_sel):
    t = LANES
    wd, wp, far = bias
    nq = seq // t
    nrange = math.gcd(nq, DSA_KEY_RANGES)
    per = nq // nrange
    outs = []
    for r in range(nrange):
        q0 = r * per
        sk = (r + 1) * per * t
        qmap = lambda b, i, q0=q0: (b, i + q0, 0)
        outs.append(pl.pallas_call(
            functools.partial(_dsa_kernel, t=t, seq=sk, k_sel=k_sel, scale=C_DIM ** -0.5, q0=q0),
            grid=(bsz, per),
            in_specs=[
                pl.BlockSpec((None, t, C_HEADS * C_DIM), qmap),
                pl.BlockSpec((None, sk, C_DIM), lambda b, i: (b, 0, C_HEADS)),
                pl.BlockSpec((None, sk, C_DIM), lambda b, i: (b, 0, C_HEADS)),
                pl.BlockSpec((None, t, C_IDX_HEADS * C_IDX_DIM), qmap),
                pl.BlockSpec((None, sk, LANES), lambda b, i: (b, 0, 4)),
                pl.BlockSpec((None, t, LANES), lambda b, i, q0=q0: (b, i + q0, 5)),
                pl.BlockSpec((C_HEADS, t, t), lambda b, i: (0, 0, 0)),
                pl.BlockSpec((C_HEADS, t, t), lambda b, i: (0, 0, 0)),
                pl.BlockSpec((C_HEADS, 1, LANES), lambda b, i: (0, 0, 0)),
            ],
            out_specs=pl.BlockSpec((None, t, C_HEADS * C_DIM), lambda b, i: (b, i, 0)),
            out_shape=jax.ShapeDtypeStruct((bsz, per * t, C_HEADS * C_DIM), BF16),
            scratch_shapes=[pltpu.VMEM((t, sk), I32), pltpu.VMEM((t, sk), F32)],
            compiler_params=_cparams("parallel", "arbitrary"),
        )(cqk, cqk, vv, idx, idx, idx, wd, wp, far))
    return outs[0] if nrange == 1 else jnp.concatenate(outs, axis=1)


def _merge_kernel(x_ref, ya_ref, yb_ref, yc_ref, g_ref, wa_ref, wb_ref, wc_ref, wo_ref, o_ref, *, d):
    merged = (g_ref[:, 0:d] * jnp.dot(ya_ref[...], wa_ref[...], preferred_element_type=F32)
              + g_ref[:, d:2 * d] * jnp.dot(yb_ref[...], wb_ref[...], preferred_element_type=F32)
              + g_ref[:, 2 * d:3 * d] * jnp.dot(yc_ref[...], wc_ref[...], preferred_element_type=F32))
    o_ref[...] = x_ref[...] + jnp.dot(merged.astype(BF16), wo_ref[...], preferred_element_type=F32)


def _merge(x, ya, yb, yc, gates, wa, wb, wc, wo):
    n, d = x.shape
    bw = ya.shape[1]
    tm = min(512, n)
    row = lambda w: pl.BlockSpec((tm, w), lambda i: (i, 0))
    full = lambda a, b: pl.BlockSpec((a, b), lambda i: (0, 0))
    return pl.pallas_call(
        functools.partial(_merge_kernel, d=d),
        grid=(n // tm,),
        in_specs=[row(d), row(bw), row(bw), row(bw), row(3 * d),
                  full(bw, d), full(bw, d), full(bw, d), full(d, d)],
        out_specs=row(d),
        out_shape=jax.ShapeDtypeStruct((n, d), F32),
        compiler_params=_cparams("parallel"),
    )(x, ya, yb, yc, gates, wa.astype(BF16), wb.astype(BF16), wc.astype(BF16), wo.astype(BF16))


def _mem_attn_kernel(x_ref, q_ref, k_ref, v_ref, wo_ref, o_ref, *, scale):
    outs = []
    for h in range(M_HEADS):
        cs = slice(h * M_DIM, (h + 1) * M_DIM)
        s = lax.dot_general(q_ref[:, cs], k_ref[:, cs], (((1,), (1,)), ((), ())),
                            preferred_element_type=F32) * scale
        p = jnp.exp(s - jnp.max(s, -1, keepdims=True))
        p = p / jnp.sum(p, -1, keepdims=True)
        outs.append(jnp.dot(p.astype(BF16), v_ref[:, cs], preferred_element_type=F32).astype(BF16))
    o = jnp.concatenate(outs, -1)
    o_ref[...] = x_ref[...] + jnp.dot(o, wo_ref[...], preferred_element_type=F32)


def _mem_attn(x, q, k, v, wo, bsz, seq, mlen):
    d = x.shape[-1]
    w = M_HEADS * M_DIM
    t = min(512, seq)
    return pl.pallas_call(
        functools.partial(_mem_attn_kernel, scale=M_DIM ** -0.5),
        grid=(bsz, seq // t),
        in_specs=[
            pl.BlockSpec((None, t, d), lambda b, i: (b, i, 0)),
            pl.BlockSpec((None, t, w), lambda b, i: (b, i, 0)),
            pl.BlockSpec((None, mlen, w), lambda b, i: (b, 0, 0)),
            pl.BlockSpec((None, mlen, w), lambda b, i: (b, 0, 0)),
            pl.BlockSpec((w, d), lambda b, i: (0, 0)),
        ],
        out_specs=pl.BlockSpec((None, t, d), lambda b, i: (b, i, 0)),
        out_shape=jax.ShapeDtypeStruct((bsz, seq, d), F32),
        compiler_params=_cparams("parallel", "arbitrary"),
    )(x, q, k, v, wo.astype(BF16))


def _topk_rows_ids(arr, ids, k):
    vals, idxs = [], []
    for _ in range(k):
        m = jnp.max(arr, axis=0, keepdims=True)
        am = jnp.min(jnp.where(arr == m, ids, 2 ** 30), axis=0, keepdims=True)
        vals.append(m)
        idxs.append(am)
        arr = jnp.where(ids == am, -jnp.inf, arr)
    return jnp.concatenate(vals, 0), jnp.concatenate(idxs, 0)


def _topk_rows(arr, k):
    return _topk_rows_ids(arr, lax.broadcasted_iota(I32, arr.shape, 0), k)


def _topk_pair_sums(v1, v2):
    k = P_TOPK
    t = v1.shape[1]
    r8 = lax.broadcasted_iota(I32, (8, t), 0)
    r16 = lax.broadcasted_iota(I32, (16, t), 0)
    vals = [v1[0:1] + v2]
    ids = [r16]
    for a in (1, 2, 3):
        vals.append(v1[a:a + 1] + v2[0:8])
        ids.append(a * k + r8)
    for b in (0, 1, 2):
        vals.append(jnp.where(r8 >= 4, v1[0:8] + v2[b:b + 1], -jnp.inf))
        ids.append(r8 * k + b)
    vals.append(v1[8:16] + v2[0:1])
    ids.append((r8 + 8) * k)
    return _topk_rows_ids(jnp.concatenate(vals, 0), jnp.concatenate(ids, 0), k)


def _pick_rows(table, sel):
    out = jnp.zeros(sel.shape, table.dtype)
    for r in range(table.shape[0]):
        out = jnp.where(sel == r, table[r:r + 1, :], out)
    return out


def _peer_topk_kernel(q_ref, k1_ref, k2_ref, ids_ref, gate_ref, *, heads):
    half = P_QDIM // 2
    dn = (((1,), (1,)), ((), ()))
    for h in range(heads):
        q1 = q_ref[:, h * P_QDIM:h * P_QDIM + half].astype(BF16)
        q2 = q_ref[:, h * P_QDIM + half:(h + 1) * P_QDIM].astype(BF16)
        s1 = lax.dot_general(k1_ref[...], q1, dn, preferred_element_type=F32)
        s2 = lax.dot_general(k2_ref[...], q2, dn, preferred_element_type=F32)
        v1, i1 = _topk_rows(s1, P_TOPK)
        v2, i2 = _topk_rows(s2, P_TOPK)
        cs, ci = _topk_pair_sums(v1, v2)
        e1 = _pick_rows(i1, ci >> 4)
        e2 = _pick_rows(i2, ci & (P_TOPK - 1))
        rows = slice(h * P_TOPK, (h + 1) * P_TOPK)
        ids_ref[rows, :] = e1 * P_NKEYS + e2
        p = jnp.exp(cs - jnp.max(cs, axis=0, keepdims=True))
        gate_ref[rows, :] = p / jnp.sum(p, axis=0, keepdims=True)


def _peer_topk(q, sk1, sk2):
    n = q.shape[0]
    t = LANES
    assert P_TOPK == 16
    hb = PEER_TOPK_HEADS_PER_STEP
    return pl.pallas_call(
        functools.partial(_peer_topk_kernel, heads=hb),
        grid=(n // t, P_HEADS // hb),
        in_specs=[
            pl.BlockSpec((t, hb * P_QDIM), lambda i, h: (i, h)),
            pl.BlockSpec((P_NKEYS, P_QDIM // 2), lambda i, h: (0, 0)),
            pl.BlockSpec((P_NKEYS, P_QDIM // 2), lambda i, h: (0, 0)),
        ],
        out_specs=[
            pl.BlockSpec((hb * P_TOPK, t), lambda i, h: (h, i)),
            pl.BlockSpec((hb * P_TOPK, t), lambda i, h: (h, i)),
        ],
        out_shape=[
            jax.ShapeDtypeStruct((P_HEADS * P_TOPK, n), I32),
            jax.ShapeDtypeStruct((P_HEADS * P_TOPK, n), F32),
        ],
        compiler_params=_cparams("parallel", "arbitrary"),
    )(q, sk1.astype(BF16), sk2.astype(BF16))


PEER_TOKENS_PER_STEP = 128
PEER_TOKENS_PER_SLOT = 4
PEER_SLOTS = 4
PEER_LOOKAHEAD = 2


def _pack_tables(u_tab, v_tab):
    ub = lax.bitcast_convert_type(u_tab.astype(BF16), jnp.uint16).astype(jnp.uint32)
    vb = lax.bitcast_convert_type(v_tab.astype(BF16), jnp.uint16).astype(jnp.uint32)
    return (ub | (vb << 16)).reshape(-1, LANES)


def _peer_expert_kernel(ids_hbm, x_ref, g_ref, gate_ref, gsum_ref, gexp_ref, uv_hbm, o_ref,
                        ids_smem, xn_ref, *scratch, tt, tb, npair, nsteps, nbuf, look):
    bufs = scratch[:nbuf]
    sem_ids, sem_row = scratch[nbuf:]
    i = pl.program_id(0)
    rows = tb * npair
    nsub = tt // tb
    m = tt * npair
    sub = x_ref.shape[1] // LANES
    cur = i & 1
    has_next = i + 1 < nsteps

    def ids_copy(step, half):
        return pltpu.make_async_copy(ids_hbm.at[step], ids_smem.at[pl.ds(half * m, m)], sem_ids.at[half])

    def issue_token(ids_base, j, dst, slot):
        tok_base = ids_base + j * npair
        for r in range(npair):
            e = ids_smem[tok_base + r]
            pltpu.make_async_copy(uv_hbm.at[pl.ds(pl.multiple_of(e * sub, sub), sub)],
                                  dst.at[pl.ds(pl.multiple_of((j * npair + r) * sub, sub), sub)],
                                  sem_row.at[slot]).start(priority=r % 2)

    def wait_rows(slot):
        pltpu.make_async_copy(bufs[(slot + 1) % nbuf], bufs[slot], sem_row.at[slot]).wait()

    @pl.when(i == 0)
    def _():
        first = ids_copy(0, 0)
        first.start()
        first.wait()
        for s in range(look):
            def body(j, carry, s=s):
                issue_token(s * rows, j, bufs[s], s)
                return carry
            lax.fori_loop(0, tb, body, 0)

    @pl.when(has_next)
    def _():
        ids_copy(i + 1, 1 - cur).start()

    x = x_ref[...]
    xn_ref[...] = x * lax.rsqrt(jnp.mean(x * x, axis=-1, keepdims=True) + EPS) * g_ref[...]

    def row_tile(ref, tok):
        row = ref[pl.ds(tok, 1), :]
        return jnp.concatenate([row[:, s * LANES:(s + 1) * LANES] for s in range(sub)], axis=0)
    ncol = npair * sub
    diag = (lax.broadcasted_iota(I32, (sub, ncol), 1) & (sub - 1)) == lax.broadcasted_iota(I32, (sub, ncol), 0)

    def sub_batch(sb, slot, gates):
        nslot = (slot + look) % nbuf
        src, dst = bufs[slot], bufs[nslot]
        wait_rows(slot)
        wraps = sb + look >= nsub

        @pl.when(jnp.logical_and(sb == nsub - look, has_next))
        def _():
            ids_copy(i + 1, 1 - cur).wait()

        nxt_half = jnp.where(jnp.logical_and(wraps, has_next), 1 - cur, cur)
        nxt_sb = jnp.where(wraps, jnp.where(has_next, sb + look - nsub, sb), sb + look)
        ids_base = nxt_half * m + nxt_sb * rows

        t0 = sb * tb
        parts = []
        for j in range(tb):
            issue_token(ids_base, j, dst, nslot)
            w = src[j * ncol:(j + 1) * ncol, :]
            u = pltpu.bitcast(w << 16, F32).astype(BF16)
            xt = row_tile(xn_ref, t0 + j).astype(BF16)
            pt = lax.dot_general(xt, u, (((1,), (1,)), ((), ())), preferred_element_type=F32)
            parts.append(jnp.where(diag, pt, 0.0))
        pm = jnp.concatenate(parts, 0)
        hi = pm.astype(BF16)
        lo = (pm - hi.astype(F32)).astype(BF16)
        h = (jnp.dot(hi, gsum_ref[...], preferred_element_type=F32)
             + jnp.dot(lo, gsum_ref[...], preferred_element_type=F32))
        h = jnp.sum(h.reshape(tb, sub, npair), axis=1)
        coef = (gates * _gelu(h)).astype(BF16)
        cexp = jnp.dot(coef, gexp_ref[...], preferred_element_type=F32)
        for j in range(tb):
            cs = jnp.where(diag, jnp.broadcast_to(cexp[j:j + 1], (sub, ncol)), 0.0).astype(BF16)
            w = src[j * ncol:(j + 1) * ncol, :]
            v = pltpu.bitcast(w & jnp.uint32(0xFFFF0000), F32).astype(BF16)
            out = row_tile(x_ref, t0 + j) + jnp.dot(cs, v, preferred_element_type=F32)
            o_ref[pl.ds(t0 + j, 1), :] = jnp.concatenate([out[s:s + 1, :] for s in range(sub)], axis=1)

    def group(gi, carry):
        g0 = pl.multiple_of(gi * (nbuf * tb), nbuf * tb)
        gates = gate_ref[pl.ds(g0, nbuf * tb), :]
        for k in range(nbuf):
            sub_batch(gi * nbuf + k, k, gates[k * tb:(k + 1) * tb])
        return carry

    lax.fori_loop(0, nsub // nbuf, group, 0)

    @pl.when(i == nsteps - 1)
    def _():
        for s in range(look):
            wait_rows(s)


def _peer_expert(x, g, ids_t, gates_t, uv_tab):
    n, d = x.shape
    npair = ids_t.shape[0]
    sub = d // LANES
    tt, tb = PEER_TOKENS_PER_STEP, PEER_TOKENS_PER_SLOT
    nbuf, look = PEER_SLOTS, PEER_LOOKAHEAD
    assert n % tt == 0 and tt % (nbuf * tb) == 0 and 0 < look < nbuf and sub == 8
    nsteps = n // tt
    ids = ids_t.T.reshape(nsteps, tt * npair)
    ncol = npair * sub
    gsum = (jnp.arange(ncol, dtype=I32)[:, None] // sub == jnp.arange(npair, dtype=I32)[None, :]).astype(BF16)
    tile = lambda: pl.BlockSpec((tt, d), lambda i: (i, 0))
    out = pl.pallas_call(
        functools.partial(_peer_expert_kernel, tt=tt, tb=tb, npair=npair, nsteps=nsteps,
                          nbuf=nbuf, look=look),
        grid=(nsteps,),
        in_specs=[
            pl.BlockSpec(memory_space=pl.ANY),
            tile(),
            pl.BlockSpec((1, d), lambda i: (0, 0)),
            pl.BlockSpec((tt, npair), lambda i: (i, 0)),
            pl.BlockSpec((ncol, npair), lambda i: (0, 0)),
            pl.BlockSpec((npair, ncol), lambda i: (0, 0)),
            pl.BlockSpec(memory_space=pl.ANY),
        ],
        out_specs=tile(),
        out_shape=jax.ShapeDtypeStruct((n, d), F32),
        scratch_shapes=[
            pltpu.SMEM((2 * tt * npair,), I32),
            pltpu.VMEM((tt, d), F32),
        ] + [pltpu.VMEM((tb * ncol, LANES), jnp.uint32)] * nbuf + [
            pltpu.SemaphoreType.DMA((2,)),
            pltpu.SemaphoreType.DMA((nbuf,)),
        ],
        compiler_params=_cparams("arbitrary"),
    )(ids, x, g.reshape(1, d).astype(F32), gates_t.T, gsum, gsum.T, uv_tab)
    return out


def _tile_gain(g, reps):
    return jnp.tile(g.astype(F32), reps)


def _layer(l, x, memn_kv, bias_a, bias_c, p):
    bsz, seq, d = x.shape
    n = bsz * seq
    lam_init = 0.8 - 0.6 * math.exp(-0.3 * l)
    x2 = x.reshape(n, d)
    w_in = p["w_in"][l]
    cols = np.cumsum([0, 512, 512, 512, 512, 512, 512, 128, 128, 512, 64, 8, 3 * d])
    seg = lambda a, b: w_in[:, cols[a]:cols[b]]
    gmix = p["norm_mix"][l]

    qk = _proj(x2, gmix, seg(0, 2), "norm64",
               jnp.concatenate([_tile_gain(p["a_q_gain"][l], 8), _tile_gain(p["a_k_gain"][l], 8)]), BF16)
    vv = _proj(x2, gmix, jnp.concatenate([seg(2, 3), seg(7, 8)], 1), "none", None, BF16, tn=640)
    u = _proj(x2, gmix, seg(3, 4), "gelu", None, F32)
    v = _proj(x2, gmix, seg(4, 5), "gelu_norm128", p["b_v_gain"][l], BF16)
    cqk = _proj(x2, gmix, seg(5, 7), "norm128",
                jnp.concatenate([_tile_gain(p["c_q_gain"][l], 4), p["c_k_gain"][l].astype(F32)]), BF16, tn=640)
    w_idx = jnp.concatenate([seg(8, 9), seg(9, 10), seg(9, 10), seg(10, 11),
                             jnp.zeros((d, 120), w_in.dtype)], 1)
    idx = _proj(x2, gmix, w_idx, "none", None, F32, tn=768)
    gates = _proj(x2, gmix, seg(11, 12), "sigmoid", None, F32)

    lamv = jnp.zeros((8, LANES), F32)
    for r, name in enumerate(("a_lq1", "a_lk1", "a_lq2", "a_lk2")):
        lamv = lamv.at[r, :A_DIM].set(p[name][l].astype(F32))
    ya = _diff_attn(qk.reshape(bsz, seq, -1), vv.reshape(bsz, seq, -1), lamv, bias_a,
                    p["a_subln_gain"][l], lam_init, bsz, seq)
    yb = _gmlp(u, v, p["b_w_s"][l], p["b_b_s"][l])
    k_sel = min(C_TOPK_MAX, seq // 4)
    yc = _dsa(cqk.reshape(bsz, seq, -1), vv.reshape(bsz, seq, -1), idx.reshape(bsz, seq, -1),
              bias_c, bsz, seq, k_sel)
    x2 = _merge(x2, ya.reshape(n, -1), yb, yc.reshape(n, -1), gates,
                p["w_br_a"][l], p["w_br_b"][l], p["w_br_c"][l], p["w_mix_out"][l])

    mk, mv = memn_kv
    mq = _proj(x2, p["norm_mem"][l], p["m_wq"][l], "norm128", _tile_gain(p["m_q_gain"][l], M_HEADS), BF16)
    x3 = _mem_attn(x2.reshape(bsz, seq, d), mq.reshape(bsz, seq, -1), mk, mv, p["m_wo"][l],
                   bsz, seq, mk.shape[1])
    x2 = x3.reshape(n, d)

    pq = _proj(x2, p["norm_peer"][l], p["p_wq"][l], "none", None, F32)
    ids_t, gates_t = _peer_topk(pq, p["p_subkey1"][l], p["p_subkey2"][l])
    x2 = _peer_expert(x2, p["norm_peer"][l], ids_t, gates_t, _pack_tables(p["p_u"][l], p["p_v"][l]))
    return x2.reshape(bsz, seq, d)


def kernel(x, mem, rel_bias, norm_mix, w_in, a_q_gain, a_k_gain, a_lq1, a_lk1, a_lq2, a_lk2, a_subln_gain, b_v_gain, b_w_s, b_b_s, c_q_gain, c_k_gain, w_br_a, w_br_b, w_br_c, w_mix_out, norm_mem, norm_memsrc, m_wq, m_wkv, m_q_gain, m_k_gain, m_wo, norm_peer, p_wq, p_subkey1, p_subkey2, p_u, p_v):
    p = dict(norm_mix=norm_mix, w_in=w_in, a_q_gain=a_q_gain, a_k_gain=a_k_gain, a_lq1=a_lq1,
             a_lk1=a_lk1, a_lq2=a_lq2, a_lk2=a_lk2, a_subln_gain=a_subln_gain, b_v_gain=b_v_gain,
             b_w_s=b_w_s, b_b_s=b_b_s, c_q_gain=c_q_gain, c_k_gain=c_k_gain, w_br_a=w_br_a,
             w_br_b=w_br_b, w_br_c=w_br_c, w_mix_out=w_mix_out, norm_mem=norm_mem, m_wq=m_wq,
             m_q_gain=m_q_gain, m_wo=m_wo, norm_peer=norm_peer, p_wq=p_wq, p_subkey1=p_subkey1,
             p_subkey2=p_subkey2, p_u=p_u, p_v=p_v)
    bsz, seq, d = x.shape
    mlen = mem.shape[1]
    depth = w_in.shape[0]
    bias_a = _bias_tiles(rel_bias, slice(0, A_HEADS), min(256, seq), seq)
    bias_c = _bias_tiles(rel_bias, slice(A_HEADS, A_HEADS + C_HEADS), LANES, seq)
    mem2 = mem.reshape(bsz * mlen, d)
    w = M_HEADS * M_DIM
    for l in range(depth):
        mk = _proj(mem2, norm_memsrc[l], m_wkv[l][:, :w], "norm128", _tile_gain(m_k_gain[l], M_HEADS), BF16)
        mv = _proj(mem2, norm_memsrc[l], m_wkv[l][:, w:], "none", None, BF16)
        x = _layer(l, x, (mk.reshape(bsz, mlen, w), mv.reshape(bsz, mlen, w)), bias_a, bias_c, p)
    return x
```

```python
import functools
import math

import jax
import jax.numpy as jnp
import numpy as np
from jax import lax
from jax.experimental import pallas as pl
from jax.experimental.pallas import tpu as pltpu

F32 = jnp.float32
BF16 = jnp.bfloat16
I32 = jnp.int32

EPS = 1e-6
NEG = -1e30
INT_MIN = -(2 ** 31)

A_HEADS = 4
A_DIM = 64
B_GROUPS = 4
B_GROUP_DIM = 128
B_CHUNK = 128
C_HEADS = 4
C_DIM = 128
C_IDX_HEADS = 8
C_IDX_DIM = 64
C_TOPK_MAX = 256
M_HEADS = 4
M_DIM = 128
P_HEADS = 8
P_QDIM = 256
P_NKEYS = 128
P_TOPK = 16
REL_BUCKETS = 32
REL_MAX_DIST = 128
LANES = 128

DSA_KEY_RANGES = 16
PEER_TOPK_HEADS_PER_STEP = 8
PEER_TOPK_TOKENS_PER_STEP = 512
DIFF_TILE = 512
DSA_TILE = 256

VMEM_LIMIT = 56 * 1024 * 1024


def _cparams(*sem):
    return pltpu.CompilerParams(dimension_semantics=sem, vmem_limit_bytes=VMEM_LIMIT)


def _gelu(x):
    return 0.5 * x * (1.0 + lax.erf(x * (1.0 / math.sqrt(2.0))))


def _group_norm(h, gain, group):
    parts = []
    for c in range(h.shape[-1] // LANES):
        hc = h[:, c * LANES:(c + 1) * LANES]
        sq = hc * hc
        if group == LANES:
            r = lax.rsqrt(jnp.sum(sq, -1, keepdims=True) * (1.0 / LANES) + EPS)
        else:
            lo = lax.broadcasted_iota(I32, hc.shape, 1) < 64
            s_lo = jnp.sum(jnp.where(lo, sq, 0.0), -1, keepdims=True)
            s_hi = jnp.sum(jnp.where(lo, 0.0, sq), -1, keepdims=True)
            r = jnp.where(lo, lax.rsqrt(s_lo * (1.0 / 64) + EPS), lax.rsqrt(s_hi * (1.0 / 64) + EPS))
        parts.append(hc * r)
    out = parts[0] if len(parts) == 1 else jnp.concatenate(parts, -1)
    return out * gain


def _epilogue(h, gain, mode):
    if mode == "none":
        return h
    if mode == "norm64":
        return _group_norm(h, gain, 64)
    if mode == "norm128":
        return _group_norm(h, gain, 128)
    if mode == "gelu":
        return _gelu(h)
    if mode == "gelu_norm128":
        return _group_norm(_gelu(h), gain, 128)
    if mode == "sigmoid":
        return jax.nn.sigmoid(h)
    raise ValueError(mode)


def _proj_kernel(x_ref, g_ref, w_ref, e_ref, o_ref, xn_ref, *, mode):
    @pl.when(pl.program_id(1) == 0)
    def _():
        x = x_ref[...]
        ms = jnp.mean(x * x, axis=-1, keepdims=True)
        xn_ref[...] = (x * lax.rsqrt(ms + EPS) * g_ref[...]).astype(BF16)

    h = jnp.dot(xn_ref[...], w_ref[...], preferred_element_type=F32)
    o_ref[...] = _epilogue(h, e_ref[...], mode).astype(o_ref.dtype)


def _proj(x, g, w, mode="none", gain=None, out_dtype=F32, tn=512):
    n, d = x.shape
    dout = w.shape[1]
    tm = min(2048, n)
    assert n % tm == 0 and dout % tn == 0, (n, dout, tn)
    if gain is None:
        gain = jnp.ones((dout,), F32)
    return pl.pallas_call(
        functools.partial(_proj_kernel, mode=mode),
        grid=(n // tm, dout // tn),
        in_specs=[
            pl.BlockSpec((tm, d), lambda i, j: (i, 0)),
            pl.BlockSpec((1, d), lambda i, j: (0, 0)),
            pl.BlockSpec((d, tn), lambda i, j: (0, j)),
            pl.BlockSpec((1, tn), lambda i, j: (0, j)),
        ],
        out_specs=pl.BlockSpec((tm, tn), lambda i, j: (i, j)),
        out_shape=jax.ShapeDtypeStruct((n, dout), out_dtype),
        scratch_shapes=[pltpu.VMEM((tm, d), BF16)],
        compiler_params=_cparams("parallel", "arbitrary"),
    )(x, g.reshape(1, d).astype(F32), w.astype(BF16), gain.reshape(1, dout).astype(F32))


def _t5_bucket(dist):
    n = jnp.maximum(dist, 0)
    max_exact = REL_BUCKETS // 2
    nf = jnp.maximum(n, 1).astype(F32)
    large = max_exact + (jnp.log(nf / max_exact) / math.log(REL_MAX_DIST / max_exact)
                         * (REL_BUCKETS - max_exact)).astype(I32)
    large = jnp.minimum(large, REL_BUCKETS - 1)
    return jnp.where(n < max_exact, n, large)


def _far_bucket_is_constant(t, seq):
    d = np.arange(t + 1, max(seq, t + 2), dtype=np.float64)
    max_exact = REL_BUCKETS // 2
    large = max_exact + np.floor(np.log(d / max_exact) / math.log(REL_MAX_DIST / max_exact)
                                 * (REL_BUCKETS - max_exact) + 1e-6).astype(np.int64)
    safe = max_exact + np.floor(np.log(d / max_exact) / math.log(REL_MAX_DIST / max_exact)
                                * (REL_BUCKETS - max_exact) - 1e-3).astype(np.int64)
    return bool(np.all(np.minimum(large, REL_BUCKETS - 1) == REL_BUCKETS - 1)
                and np.all(np.minimum(safe, REL_BUCKETS - 1) == REL_BUCKETS - 1))


def _bias_tiles(rel_bias, heads, t, seq):
    assert _far_bucket_is_constant(t, seq)
    tab = rel_bias[_t5_bucket(jnp.arange(2 * t, dtype=I32))][:, heads].astype(F32).T
    nh = tab.shape[0]
    period = 2 * t

    def toeplitz(w):
        rep = jnp.tile(w, (1, t))[:, :t * (period - 1)]
        return rep.reshape(nh, t, period - 1)[:, :, :t]

    back = tab[:, 1:t][:, ::-1]
    w_diag = jnp.concatenate([tab[:, 0:1], jnp.full((nh, t), NEG, F32), back], axis=1)
    w_prev = jnp.concatenate([tab[:, 1:t + 1][:, ::-1], tab[:, t:t + 1], tab[:, t + 1:][:, ::-1]], axis=1)
    far = rel_bias[REL_BUCKETS - 1, heads].astype(F32)
    return toeplitz(w_diag), toeplitz(w_prev), jnp.broadcast_to(far[:, None, None], (nh, 1, LANES))


def _diff_attn_kernel(lam_ref, q_ref, k_ref, v_ref, wd_ref, wp_ref, c_ref, g_ref, o_ref, *,
                      t, scale, lam_init):
    i = pl.program_id(1)
    lo = lax.broadcasted_iota(I32, (t, LANES), 1) < A_DIM
    qs = []
    for h in range(A_HEADS):
        q = q_ref[:, h * LANES:(h + 1) * LANES]
        zero = jnp.zeros_like(q)
        qs.append((jnp.where(lo, q, zero), jnp.where(lo, zero, q)))

    def tile(start, bias_of, carry):
        out = []
        for h in range(A_HEADS):
            k = k_ref[pl.ds(start, t), h * LANES:(h + 1) * LANES]
            v = v_ref[pl.ds(start, t), h * LANES:(h + 1) * LANES]
            bias = bias_of(h)
            for m in range(2):
                mo, lo_, ao = carry[2 * h + m]
                s = lax.dot_general(qs[h][m], k, (((1,), (1,)), ((), ())), preferred_element_type=F32)
                s = s * scale + bias
                mn = jnp.maximum(mo, jnp.max(s, -1, keepdims=True))
                a = jnp.exp(mo - mn)
                p = jnp.exp(s - mn)
                ln = a * lo_ + jnp.sum(p, -1, keepdims=True)
                an = a * ao + jnp.dot(p.astype(BF16), v, preferred_element_type=F32)
                out.append((mn, ln, an))
        return tuple(out)

    one = (jnp.full((t, 1), NEG, F32), jnp.zeros((t, 1), F32), jnp.zeros((t, LANES), F32))
    carry = lax.fori_loop(0, jnp.maximum(i - 1, 0),
                          lambda j, cr: tile(pl.multiple_of(j * t, t), lambda h: c_ref[h, 0:1, 0:1], cr),
                          (one,) * (2 * A_HEADS))
    jp = jnp.maximum(i - 1, 0)
    carry = tile(pl.multiple_of(jp * t, t), lambda h: jnp.where(i > 0, wp_ref[h], NEG), carry)
    carry = tile(pl.multiple_of(i * t, t), lambda h: wd_ref[h], carry)

    lv = lam_ref[...]
    lam = (jnp.exp(jnp.sum(lv[0:1] * lv[1:2], -1, keepdims=True))
           - jnp.exp(jnp.sum(lv[2:3] * lv[3:4], -1, keepdims=True)) + lam_init)
    for h in range(A_HEADS):
        c1, c2 = carry[2 * h], carry[2 * h + 1]
        o = c1[2] / c1[1] - lam * (c2[2] / c2[1])
        r = lax.rsqrt(jnp.mean(o * o, -1, keepdims=True) + EPS)
        o_ref[:, h * LANES:(h + 1) * LANES] = ((o * r * g_ref[...]) * (1.0 - lam_init)).astype(o_ref.dtype)


def _diff_attn(qk, vv, lamv, bias, gain, lam_init, bsz, seq):
    t = min(DIFF_TILE, seq)
    nq = seq // t
    w = A_HEADS * LANES
    wd, wp, far = bias
    return pl.pallas_call(
        functools.partial(_diff_attn_kernel, t=t, scale=A_DIM ** -0.5, lam_init=lam_init),
        grid=(bsz, nq),
        in_specs=[
            pl.BlockSpec((8, LANES), lambda b, i: (0, 0)),
            pl.BlockSpec((None, t, w), lambda b, i: (b, i, 0)),
            pl.BlockSpec((None, seq, w), lambda b, i: (b, 0, 1)),
            pl.BlockSpec((None, seq, vv.shape[-1]), lambda b, i: (b, 0, 0)),
            pl.BlockSpec((A_HEADS, t, t), lambda b, i: (0, 0, 0)),
            pl.BlockSpec((A_HEADS, t, t), lambda b, i: (0, 0, 0)),
            pl.BlockSpec((A_HEADS, 1, LANES), lambda b, i: (0, 0, 0)),
            pl.BlockSpec((1, LANES), lambda b, i: (0, 0)),
        ],
        out_specs=pl.BlockSpec((None, t, w), lambda b, i: (b, i, 0)),
        out_shape=jax.ShapeDtypeStruct((bsz, seq, w), BF16),
        compiler_params=_cparams("parallel", "arbitrary"),
    )(lamv, qk, qk, vv, wd, wp, far, gain.reshape(1, LANES).astype(F32))


def _gmlp_kernel(u_ref, v_ref, w_ref, b_ref, o_ref, *, chunks):
    row = lax.broadcasted_iota(I32, (B_CHUNK, B_CHUNK), 0)
    col = lax.broadcasted_iota(I32, (B_CHUNK, B_CHUNK), 1)
    ws = [jnp.where(row >= col, w_ref[g], 0.0).astype(BF16) for g in range(B_GROUPS)]
    for c in range(chunks):
        rs = slice(c * B_CHUNK, (c + 1) * B_CHUNK)
        for g in range(B_GROUPS):
            cs = slice(g * B_GROUP_DIM, (g + 1) * B_GROUP_DIM)
            sv = jnp.dot(ws[g], v_ref[rs, cs], preferred_element_type=F32) + b_ref[:, cs]
            o_ref[rs, cs] = (u_ref[rs, cs] * sv).astype(o_ref.dtype)


def _gmlp(u, v, w_s, b_s):
    n, width = u.shape
    chunks = 4
    tm = chunks * B_CHUNK
    bfull = jnp.repeat(b_s.T.astype(F32), B_GROUP_DIM, axis=1)
    return pl.pallas_call(
        functools.partial(_gmlp_kernel, chunks=chunks),
        grid=(n // tm,),
        in_specs=[
            pl.BlockSpec((tm, width), lambda i: (i, 0)),
            pl.BlockSpec((tm, width), lambda i: (i, 0)),
            pl.BlockSpec((B_GROUPS, B_CHUNK, B_CHUNK), lambda i: (0, 0, 0)),
            pl.BlockSpec((B_CHUNK, width), lambda i: (0, 0)),
        ],
        out_specs=pl.BlockSpec((tm, width), lambda i: (i, 0)),
        out_shape=jax.ShapeDtypeStruct((n, width), BF16),
        compiler_params=_cparams("parallel"),
    )(u, v, w_s.astype(F32), bfull)


def _dsa_kernel(cq_ref, ck_ref, cv_ref, iq_ref, ikk_ref, iwq_ref, wd_ref, wp_ref, c_ref, o_ref,
                key_ref, sel_ref, *, t, seq, k_sel, scale, q0):
    i = pl.program_id(1) + q0
    nt = seq // t
    qpos = i * t + lax.broadcasted_iota(I32, (t, seq), 0)
    kpos = lax.broadcasted_iota(I32, (t, seq), 1)
    valid = kpos <= qpos

    ikk = ikk_ref[...].astype(BF16)
    iw = iwq_ref[...]
    score = jnp.zeros((t, seq), F32)
    for pair in range(C_IDX_HEADS // 2):
        qp = iq_ref[:, pair * LANES:(pair + 1) * LANES].astype(BF16)
        lo = lax.broadcasted_iota(I32, qp.shape, 1) < C_IDX_DIM
        zero = jnp.zeros_like(qp)
        for half in range(2):
            h = 2 * pair + half
            qh = jnp.where(lo, qp, zero) if half == 0 else jnp.where(lo, zero, qp)
            d = lax.dot_general(qh, ikk, (((1,), (1,)), ((), ())), preferred_element_type=F32)
            score = score + jnp.maximum(d, 0.0) * iw[:, h:h + 1]

    bits = pltpu.bitcast(score + 0.0, I32)
    skey = bits ^ ((bits >> 31) & 0x7FFFFFFF)
    key_ref[...] = jnp.where(valid, skey, INT_MIN)

    def count(mask):
        return jnp.sum(jnp.where(mask, 1.0, 0.0), axis=-1, keepdims=True)

    def vbody(it, p_u):
        cand = p_u | (jnp.int32(1) << (31 - it))
        cnt = count(key_ref[...] >= (cand ^ INT_MIN))
        return jnp.where(cnt >= k_sel, cand, p_u)

    p_u = lax.fori_loop(0, 32, vbody, jnp.zeros((t, 1), I32))
    thr = p_u ^ INT_MIN
    keys = key_ref[...]
    gt = keys > thr
    eq = (keys == thr) & valid
    need = k_sel - count(gt)
    sel_ref[...] = jnp.where(valid & (keys >= thr), 1.0, 0.0)

    nbits = max(1, (seq - 1).bit_length())

    @pl.when(jnp.max(count(eq) - need) > 0.0)
    def _():
        def ibody(it, m):
            cand = m | (jnp.int32(1) << (nbits - 1 - it))
            cnt = count(eq & (kpos < cand))
            return jnp.where(cnt < need, cand, m)

        m = lax.fori_loop(0, nbits, ibody, jnp.zeros((t, 1), I32))
        sel_ref[...] = jnp.where(valid & (gt | (eq & (kpos <= m))), 1.0, 0.0)

    sel = sel_ref[...] > 0.5

    ck = ck_ref[...]
    cv = cv_ref[...]
    for h in range(C_HEADS):
        far = c_ref[h, 0:1, 0:1]
        tiles = []
        for jt in range(nt):
            tiles.append(jnp.where(i == jt, wd_ref[h], jnp.where(i == jt + 1, wp_ref[h], far)))
        bias = tiles[0] if nt == 1 else jnp.concatenate(tiles, -1)
        s = lax.dot_general(cq_ref[:, h * C_DIM:(h + 1) * C_DIM], ck, (((1,), (1,)), ((), ())),
                            preferred_element_type=F32)
        s = jnp.where(sel, s * scale + bias, NEG)
        p = jnp.exp(s - jnp.max(s, -1, keepdims=True))
        p = p / jnp.sum(p, -1, keepdims=True)
        o_ref[:, h * C_DIM:(h + 1) * C_DIM] = jnp.dot(
            p.astype(BF16), cv, preferred_element_type=F32).astype(o_ref.dtype)


def _dsa(cqk, vv, idx, bias, bsz, seq, k_sel):
    t = min(DSA_TILE, seq)
    wd, wp, far = bias
    nq = seq // t
    nrange = math.gcd(nq, DSA_KEY_RANGES)
    per = nq // nrange
    outs = []
    for r in range(nrange):
        q0 = r * per
        sk = (r + 1) * per * t
        qmap = lambda b, i, q0=q0: (b, i + q0, 0)
        outs.append(pl.pallas_call(
            functools.partial(_dsa_kernel, t=t, seq=sk, k_sel=k_sel, scale=C_DIM ** -0.5, q0=q0),
            grid=(bsz, per),
            in_specs=[
                pl.BlockSpec((None, t, C_HEADS * C_DIM), qmap),
                pl.BlockSpec((None, sk, C_DIM), lambda b, i: (b, 0, C_HEADS)),
                pl.BlockSpec((None, sk, C_DIM), lambda b, i: (b, 0, C_HEADS)),
                pl.BlockSpec((None, t, C_IDX_HEADS * C_IDX_DIM), qmap),
                pl.BlockSpec((None, sk, LANES), lambda b, i: (b, 0, 4)),
                pl.BlockSpec((None, t, LANES), lambda b, i, q0=q0: (b, i + q0, 5)),
                pl.BlockSpec((C_HEADS, t, t), lambda b, i: (0, 0, 0)),
                pl.BlockSpec((C_HEADS, t, t), lambda b, i: (0, 0, 0)),
                pl.BlockSpec((C_HEADS, 1, LANES), lambda b, i: (0, 0, 0)),
            ],
            out_specs=pl.BlockSpec((None, t, C_HEADS * C_DIM), lambda b, i: (b, i, 0)),
            out_shape=jax.ShapeDtypeStruct((bsz, per * t, C_HEADS * C_DIM), BF16),
            scratch_shapes=[pltpu.VMEM((t, sk), I32), pltpu.VMEM((t, sk), F32)],
            compiler_params=_cparams("parallel", "arbitrary"),
        )(cqk, cqk, vv, idx, idx, idx, wd, wp, far))
    return outs[0] if nrange == 1 else jnp.concatenate(outs, axis=1)


def _merge_kernel(x_ref, ya_ref, yb_ref, yc_ref, g_ref, wa_ref, wb_ref, wc_ref, wo_ref, o_ref, *, d):
    merged = (g_ref[:, 0:d] * jnp.dot(ya_ref[...], wa_ref[...], preferred_element_type=F32)
              + g_ref[:, d:2 * d] * jnp.dot(yb_ref[...], wb_ref[...], preferred_element_type=F32)
              + g_ref[:, 2 * d:3 * d] * jnp.dot(yc_ref[...], wc_ref[...], preferred_element_type=F32))
    o_ref[...] = x_ref[...] + jnp.dot(merged.astype(BF16), wo_ref[...], preferred_element_type=F32)


def _merge(x, ya, yb, yc, gates, wa, wb, wc, wo):
    n, d = x.shape
    bw = ya.shape[1]
    tm = min(1024, n)
    row = lambda w: pl.BlockSpec((tm, w), lambda i: (i, 0))
    full = lambda a, b: pl.BlockSpec((a, b), lambda i: (0, 0))
    return pl.pallas_call(
        functools.partial(_merge_kernel, d=d),
        grid=(n // tm,),
        in_specs=[row(d), row(bw), row(bw), row(bw), row(3 * d),
                  full(bw, d), full(bw, d), full(bw, d), full(d, d)],
        out_specs=row(d),
        out_shape=jax.ShapeDtypeStruct((n, d), F32),
        compiler_params=_cparams("parallel"),
    )(x, ya, yb, yc, gates, wa.astype(BF16), wb.astype(BF16), wc.astype(BF16), wo.astype(BF16))


def _mem_attn_kernel(x_ref, q_ref, k_ref, v_ref, wo_ref, o_ref, *, scale):
    outs = []
    for h in range(M_HEADS):
        cs = slice(h * M_DIM, (h + 1) * M_DIM)
        s = lax.dot_general(q_ref[:, cs], k_ref[:, cs], (((1,), (1,)), ((), ())),
                            preferred_element_type=F32) * scale
        p = jnp.exp(s - jnp.max(s, -1, keepdims=True))
        p = p / jnp.sum(p, -1, keepdims=True)
        outs.append(jnp.dot(p.astype(BF16), v_ref[:, cs], preferred_element_type=F32).astype(BF16))
    o = jnp.concatenate(outs, -1)
    o_ref[...] = x_ref[...] + jnp.dot(o, wo_ref[...], preferred_element_type=F32)


def _mem_attn(x, q, k, v, wo, bsz, seq, mlen):
    d = x.shape[-1]
    w = M_HEADS * M_DIM
    t = min(512, seq)
    return pl.pallas_call(
        functools.partial(_mem_attn_kernel, scale=M_DIM ** -0.5),
        grid=(bsz, seq // t),
        in_specs=[
            pl.BlockSpec((None, t, d), lambda b, i: (b, i, 0)),
            pl.BlockSpec((None, t, w), lambda b, i: (b, i, 0)),
            pl.BlockSpec((None, mlen, w), lambda b, i: (b, 0, 0)),
            pl.BlockSpec((None, mlen, w), lambda b, i: (b, 0, 0)),
            pl.BlockSpec((w, d), lambda b, i: (0, 0)),
        ],
        out_specs=pl.BlockSpec((None, t, d), lambda b, i: (b, i, 0)),
        out_shape=jax.ShapeDtypeStruct((bsz, seq, d), F32),
        compiler_params=_cparams("parallel", "arbitrary"),
    )(x, q, k, v, wo.astype(BF16))


def _topk_rows_ids(arr, ids, k):
    vals, idxs = [], []
    for _ in range(k):
        m = jnp.max(arr, axis=0, keepdims=True)
        am = jnp.min(jnp.where(arr == m, ids, 2 ** 30), axis=0, keepdims=True)
        vals.append(m)
        idxs.append(am)
        arr = jnp.where(ids == am, -jnp.inf, arr)
    return jnp.concatenate(vals, 0), jnp.concatenate(idxs, 0)


def _topk_rows(arr, k):
    return _topk_rows_ids(arr, lax.broadcasted_iota(I32, arr.shape, 0), k)


def _topk_pair_sums(v1, v2):
    k = P_TOPK
    t = v1.shape[1]
    r8 = lax.broadcasted_iota(I32, (8, t), 0)
    r16 = lax.broadcasted_iota(I32, (16, t), 0)
    vals = [v1[0:1] + v2]
    ids = [r16]
    for a in (1, 2, 3):
        vals.append(v1[a:a + 1] + v2[0:8])
        ids.append(a * k + r8)
    for b in (0, 1, 2):
        vals.append(jnp.where(r8 >= 4, v1[0:8] + v2[b:b + 1], -jnp.inf))
        ids.append(r8 * k + b)
    vals.append(v1[8:16] + v2[0:1])
    ids.append((r8 + 8) * k)
    return _topk_rows_ids(jnp.concatenate(vals, 0), jnp.concatenate(ids, 0), k)


def _pick_rows(table, sel):
    out = jnp.zeros(sel.shape, table.dtype)
    for r in range(table.shape[0]):
        out = jnp.where(sel == r, table[r:r + 1, :], out)
    return out


def _peer_topk_kernel(q_ref, k1_ref, k2_ref, ids_ref, gate_ref, *, heads):
    half = P_QDIM // 2
    dn = (((1,), (1,)), ((), ()))
    for h in range(heads):
        q1 = q_ref[:, h * P_QDIM:h * P_QDIM + half].astype(BF16)
        q2 = q_ref[:, h * P_QDIM + half:(h + 1) * P_QDIM].astype(BF16)
        s1 = lax.dot_general(k1_ref[...], q1, dn, preferred_element_type=F32)
        s2 = lax.dot_general(k2_ref[...], q2, dn, preferred_element_type=F32)
        v1, i1 = _topk_rows(s1, P_TOPK)
        v2, i2 = _topk_rows(s2, P_TOPK)
        cs, ci = _topk_pair_sums(v1, v2)
        e1 = _pick_rows(i1, ci >> 4)
        e2 = _pick_rows(i2, ci & (P_TOPK - 1))
        rows = slice(h * P_TOPK, (h + 1) * P_TOPK)
        ids_ref[rows, :] = e1 * P_NKEYS + e2
        p = jnp.exp(cs - jnp.max(cs, axis=0, keepdims=True))
        gate_ref[rows, :] = p / jnp.sum(p, axis=0, keepdims=True)


def _peer_topk(q, sk1, sk2):
    n = q.shape[0]
    t = min(PEER_TOPK_TOKENS_PER_STEP, n)
    assert P_TOPK == 16
    hb = PEER_TOPK_HEADS_PER_STEP
    return pl.pallas_call(
        functools.partial(_peer_topk_kernel, heads=hb),
        grid=(n // t, P_HEADS // hb),
        in_specs=[
            pl.BlockSpec((t, hb * P_QDIM), lambda i, h: (i, h)),
            pl.BlockSpec((P_NKEYS, P_QDIM // 2), lambda i, h: (0, 0)),
            pl.BlockSpec((P_NKEYS, P_QDIM // 2), lambda i, h: (0, 0)),
        ],
        out_specs=[
            pl.BlockSpec((hb * P_TOPK, t), lambda i, h: (h, i)),
            pl.BlockSpec((hb * P_TOPK, t), lambda i, h: (h, i)),
        ],
        out_shape=[
            jax.ShapeDtypeStruct((P_HEADS * P_TOPK, n), I32),
            jax.ShapeDtypeStruct((P_HEADS * P_TOPK, n), F32),
        ],
        compiler_params=_cparams("parallel", "arbitrary"),
    )(q, sk1.astype(BF16), sk2.astype(BF16))


PEER_TOKENS_PER_STEP = 128
PEER_TOKENS_PER_SLOT = 4
PEER_SLOTS = 4
PEER_LOOKAHEAD = 2


def _pack_tables(u_tab, v_tab):
    ub = lax.bitcast_convert_type(u_tab.astype(BF16), jnp.uint16).astype(jnp.uint32)
    vb = lax.bitcast_convert_type(v_tab.astype(BF16), jnp.uint16).astype(jnp.uint32)
    return (ub | (vb << 16)).reshape(-1, LANES)


def _peer_expert_kernel(ids_hbm, x_ref, g_ref, gate_ref, gsum_ref, gexp_ref, uv_hbm, o_ref,
                        ids_smem, xn_ref, *scratch, tt, tb, npair, nsteps, nbuf, look):
    bufs = scratch[:nbuf]
    sem_ids, sem_row = scratch[nbuf:]
    i = pl.program_id(0)
    rows = tb * npair
    nsub = tt // tb
    m = tt * npair
    sub = x_ref.shape[1]
    cur = i & 1
    has_next = i + 1 < nsteps

    def ids_copy(step, half):
        return pltpu.make_async_copy(ids_hbm.at[step], ids_smem.at[pl.ds(half * m, m)], sem_ids.at[half])

    def issue_token(ids_base, j, dst, slot):
        tok_base = ids_base + j * npair
        for r in range(npair):
            e = ids_smem[tok_base + r]
            pltpu.make_async_copy(uv_hbm.at[pl.ds(pl.multiple_of(e * sub, sub), sub)],
                                  dst.at[pl.ds(pl.multiple_of((j * npair + r) * sub, sub), sub)],
                                  sem_row.at[slot]).start(priority=r % 2)

    def wait_rows(slot):
        pltpu.make_async_copy(bufs[(slot + 1) % nbuf], bufs[slot], sem_row.at[slot]).wait()

    @pl.when(i == 0)
    def _():
        first = ids_copy(0, 0)
        first.start()
        first.wait()
        for s in range(look):
            def body(j, carry, s=s):
                issue_token(s * rows, j, bufs[s], s)
                return carry
            lax.fori_loop(0, tb, body, 0)

    @pl.when(has_next)
    def _():
        ids_copy(i + 1, 1 - cur).start()

    x = x_ref[...]
    ss = jnp.sum(jnp.sum(x * x, axis=2, keepdims=True), axis=1, keepdims=True)
    xn_ref[...] = x * lax.rsqrt(ss * (1.0 / (sub * LANES)) + EPS) * g_ref[...]
    ncol = npair * sub
    diag = (lax.broadcasted_iota(I32, (sub, ncol), 1) & (sub - 1)) == lax.broadcasted_iota(I32, (sub, ncol), 0)

    def sub_batch(sb, slot, gates):
        nslot = (slot + look) % nbuf
        src, dst = bufs[slot], bufs[nslot]
        wait_rows(slot)
        wraps = sb + look >= nsub

        @pl.when(jnp.logical_and(sb == nsub - look, has_next))
        def _():
            ids_copy(i + 1, 1 - cur).wait()

        nxt_half = jnp.where(jnp.logical_and(wraps, has_next), 1 - cur, cur)
        nxt_sb = jnp.where(wraps, jnp.where(has_next, sb + look - nsub, sb), sb + look)
        ids_base = nxt_half * m + nxt_sb * rows

        t0 = sb * tb
        parts = []
        for j in range(tb):
            issue_token(ids_base, j, dst, nslot)
            w = src[j * ncol:(j + 1) * ncol, :]
            u = pltpu.bitcast(w << 16, F32).astype(BF16)
            xt = xn_ref[t0 + j].astype(BF16)
            pt = lax.dot_general(xt, u, (((1,), (1,)), ((), ())), preferred_element_type=F32)
            parts.append(jnp.where(diag, pt, 0.0))
        pm = jnp.concatenate(parts, 0)
        hi = pm.astype(BF16)
        lo = (pm - hi.astype(F32)).astype(BF16)
        h = (jnp.dot(hi, gsum_ref[...], preferred_element_type=F32)
             + jnp.dot(lo, gsum_ref[...], preferred_element_type=F32))
        h = jnp.sum(h.reshape(tb, sub, npair), axis=1)
        coef = (gates * _gelu(h)).astype(BF16)
        cexp = jnp.dot(coef, gexp_ref[...], preferred_element_type=F32)
        for j in range(tb):
            cs = jnp.where(diag, jnp.broadcast_to(cexp[j:j + 1], (sub, ncol)), 0.0).astype(BF16)
            w = src[j * ncol:(j + 1) * ncol, :]
            v = pltpu.bitcast(w & jnp.uint32(0xFFFF0000), F32).astype(BF16)
            o_ref[t0 + j] = x_ref[t0 + j] + jnp.dot(cs, v, preferred_element_type=F32)

    def group(gi, carry):
        g0 = pl.multiple_of(gi * (nbuf * tb), nbuf * tb)
        gates = gate_ref[pl.ds(g0, nbuf * tb), :]
        for k in range(nbuf):
            sub_batch(gi * nbuf + k, k, gates[k * tb:(k + 1) * tb])
        return carry

    lax.fori_loop(0, nsub // nbuf, group, 0)

    @pl.when(i == nsteps - 1)
    def _():
        for s in range(look):
            wait_rows(s)


def _peer_expert(x, g, ids_t, gates_t, uv_tab):
    n, d = x.shape
    npair = ids_t.shape[0]
    sub = d // LANES
    tt, tb = PEER_TOKENS_PER_STEP, PEER_TOKENS_PER_SLOT
    nbuf, look = PEER_SLOTS, PEER_LOOKAHEAD
    assert n % tt == 0 and tt % (nbuf * tb) == 0 and 0 < look < nbuf and sub == 8
    nsteps = n // tt
    ids = ids_t.T.reshape(nsteps, tt * npair)
    ncol = npair * sub
    gsum = (jnp.arange(ncol, dtype=I32)[:, None] // sub == jnp.arange(npair, dtype=I32)[None, :]).astype(BF16)
    tile = lambda: pl.BlockSpec((tt, sub, LANES), lambda i: (i, 0, 0))
    out = pl.pallas_call(
        functools.partial(_peer_expert_kernel, tt=tt, tb=tb, npair=npair, nsteps=nsteps,
                          nbuf=nbuf, look=look),
        grid=(nsteps,),
        in_specs=[
            pl.BlockSpec(memory_space=pl.ANY),
            tile(),
            pl.BlockSpec((sub, LANES), lambda i: (0, 0)),
            pl.BlockSpec((tt, npair), lambda i: (i, 0)),
            pl.BlockSpec((ncol, npair), lambda i: (0, 0)),
            pl.BlockSpec((npair, ncol), lambda i: (0, 0)),
            pl.BlockSpec(memory_space=pl.ANY),
        ],
        out_specs=tile(),
        out_shape=jax.ShapeDtypeStruct((n, sub, LANES), F32),
        scratch_shapes=[
            pltpu.SMEM((2 * tt * npair,), I32),
            pltpu.VMEM((tt, sub, LANES), F32),
        ] + [pltpu.VMEM((tb * ncol, LANES), jnp.uint32)] * nbuf + [
            pltpu.SemaphoreType.DMA((2,)),
            pltpu.SemaphoreType.DMA((nbuf,)),
        ],
        compiler_params=_cparams("arbitrary"),
    )(ids, x.reshape(n, sub, LANES), g.reshape(sub, LANES).astype(F32), gates_t.T, gsum, gsum.T, uv_tab)
    return out.reshape(n, d)


def _tile_gain(g, reps):
    return jnp.tile(g.astype(F32), reps)


def _layer(l, x, memn_kv, bias_a, bias_c, p):
    bsz, seq, d = x.shape
    n = bsz * seq
    lam_init = 0.8 - 0.6 * math.exp(-0.3 * l)
    x2 = x.reshape(n, d)
    w_in = p["w_in"][l]
    cols = np.cumsum([0, 512, 512, 512, 512, 512, 512, 128, 128, 512, 64, 8, 3 * d])
    seg = lambda a, b: w_in[:, cols[a]:cols[b]]
    gmix = p["norm_mix"][l]

    qk = _proj(x2, gmix, seg(0, 2), "norm64",
               jnp.concatenate([_tile_gain(p["a_q_gain"][l], 8), _tile_gain(p["a_k_gain"][l], 8)]), BF16)
    vv = _proj(x2, gmix, jnp.concatenate([seg(2, 3), seg(7, 8)], 1), "none", None, BF16, tn=640)
    u = _proj(x2, gmix, seg(3, 4), "gelu", None, F32)
    v = _proj(x2, gmix, seg(4, 5), "gelu_norm128", p["b_v_gain"][l], BF16)
    cqk = _proj(x2, gmix, seg(5, 7), "norm128",
                jnp.concatenate([_tile_gain(p["c_q_gain"][l], 4), p["c_k_gain"][l].astype(F32)]), BF16, tn=640)
    w_idx = jnp.concatenate([seg(8, 9), seg(9, 10), seg(9, 10), seg(10, 11),
                             jnp.zeros((d, 120), w_in.dtype)], 1)
    idx = _proj(x2, gmix, w_idx, "none", None, F32, tn=768)
    gates = _proj(x2, gmix, seg(11, 12), "sigmoid", None, F32, tn=1024)

    lamv = jnp.zeros((8, LANES), F32)
    for r, name in enumerate(("a_lq1", "a_lk1", "a_lq2", "a_lk2")):
        lamv = lamv.at[r, :A_DIM].set(p[name][l].astype(F32))
    ya = _diff_attn(qk.reshape(bsz, seq, -1), vv.reshape(bsz, seq, -1), lamv, bias_a,
                    p["a_subln_gain"][l], lam_init, bsz, seq)
    yb = _gmlp(u, v, p["b_w_s"][l], p["b_b_s"][l])
    k_sel = min(C_TOPK_MAX, seq // 4)
    yc = _dsa(cqk.reshape(bsz, seq, -1), vv.reshape(bsz, seq, -1), idx.reshape(bsz, seq, -1),
              bias_c, bsz, seq, k_sel)
    x2 = _merge(x2, ya.reshape(n, -1), yb, yc.reshape(n, -1), gates,
                p["w_br_a"][l], p["w_br_b"][l], p["w_br_c"][l], p["w_mix_out"][l])

    mk, mv = memn_kv
    mq = _proj(x2, p["norm_mem"][l], p["m_wq"][l], "norm128", _tile_gain(p["m_q_gain"][l], M_HEADS), BF16)
    x3 = _mem_attn(x2.reshape(bsz, seq, d), mq.reshape(bsz, seq, -1), mk, mv, p["m_wo"][l],
                   bsz, seq, mk.shape[1])
    x2 = x3.reshape(n, d)

    pq = _proj(x2, p["norm_peer"][l], p["p_wq"][l], "none", None, F32, tn=1024)
    ids_t, gates_t = _peer_topk(pq, p["p_subkey1"][l], p["p_subkey2"][l])
    x2 = _peer_expert(x2, p["norm_peer"][l], ids_t, gates_t, _pack_tables(p["p_u"][l], p["p_v"][l]))
    return x2.reshape(bsz, seq, d)


def kernel(x, mem, rel_bias, norm_mix, w_in, a_q_gain, a_k_gain, a_lq1, a_lk1, a_lq2, a_lk2, a_subln_gain, b_v_gain, b_w_s, b_b_s, c_q_gain, c_k_gain, w_br_a, w_br_b, w_br_c, w_mix_out, norm_mem, norm_memsrc, m_wq, m_wkv, m_q_gain, m_k_gain, m_wo, norm_peer, p_wq, p_subkey1, p_subkey2, p_u, p_v):
    p = dict(norm_mix=norm_mix, w_in=w_in, a_q_gain=a_q_gain, a_k_gain=a_k_gain, a_lq1=a_lq1,
             a_lk1=a_lk1, a_lq2=a_lq2, a_lk2=a_lk2, a_subln_gain=a_subln_gain, b_v_gain=b_v_gain,
             b_w_s=b_w_s, b_b_s=b_b_s, c_q_gain=c_q_gain, c_k_gain=c_k_gain, w_br_a=w_br_a,
             w_br_b=w_br_b, w_br_c=w_br_c, w_mix_out=w_mix_out, norm_mem=norm_mem, m_wq=m_wq,
             m_q_gain=m_q_gain, m_wo=m_wo, norm_peer=norm_peer, p_wq=p_wq, p_subkey1=p_subkey1,
             p_subkey2=p_subkey2, p_u=p_u, p_v=p_v)
    bsz, seq, d = x.shape
    mlen = mem.shape[1]
    depth = w_in.shape[0]
    bias_a = _bias_tiles(rel_bias, slice(0, A_HEADS), min(DIFF_TILE, seq), seq)
    bias_c = _bias_tiles(rel_bias, slice(A_HEADS, A_HEADS + C_HEADS), min(DSA_TILE, seq), seq)
    mem2 = mem.reshape(bsz * mlen, d)
    w = M_HEADS * M_DIM
    for l in range(depth):
        mk = _proj(mem2, norm_memsrc[l], m_wkv[l][:, :w], "norm128", _tile_gain(m_k_gain[l], M_HEADS), BF16)
        mv = _proj(mem2, norm_memsrc[l], m_wkv[l][:, w:], "none", None, BF16)
        x = _layer(l, x, (mk.reshape(bsz, mlen, w), mv.reshape(bsz, mlen, w)), bias_a, bias_c, p)
    return x
```
